```python
import math
import jax, jax.numpy as jnp
from jax import lax
import numpy as np

D_MODEL = 1024
BATCH = 32
SEQ = 2048
DEPTH = 1

N_ATT_HEADS = 4
HEAD_DIM = 64
V_HEAD_DIM = 2 * HEAD_DIM
QK_WIDTH = N_ATT_HEADS * 2 * HEAD_DIM
ATT_WIDTH = N_ATT_HEADS * V_HEAD_DIM
POOL_WIDTH = D_MODEL // 2
POOL_WINDOWS = (2, 4, 8, 16)
N_POOL_GROUPS = len(POOL_WINDOWS)
POOL_GROUP_DIM = POOL_WIDTH // N_POOL_GROUPS
N_BRANCHES = 2
IN_WIDTH = 2 * QK_WIDTH + ATT_WIDTH + POOL_WIDTH + N_BRANCHES * D_MODEL
D_FF = 2816
ROPE_THETA = 10000.0
Q_BLOCK = 128
LN_EPS = 1e-5
RMS_EPS = 1e-5
DEEPNORM_ALPHA = (2.0 * DEPTH) ** 0.25
DEEPNORM_BETA = (8.0 * DEPTH) ** -0.25

kernel_name = "hybrid_diffattn_multiscale_pool_macaron_deepnorm"


def lambda_init_for(layer_idx):
    return 0.8 - 0.6 * math.exp(-0.3 * layer_idx)


def layer_norm(x, g, b):
    xf = x.astype(jnp.float32)
    mu = jnp.mean(xf, axis=-1, keepdims=True)
    var = jnp.mean(jnp.square(xf - mu), axis=-1, keepdims=True)
    return ((xf - mu) * lax.rsqrt(var + LN_EPS)).astype(x.dtype) * g + b


def swiglu(x, w_gate, w_up, w_down):
    return (jax.nn.silu(x @ w_gate) * (x @ w_up)) @ w_down


def rope_tables(seq, dtype):
    inv = 1.0 / (ROPE_THETA ** (jnp.arange(0, HEAD_DIM, 2, dtype=jnp.float32) / HEAD_DIM))
    ang = jnp.arange(seq, dtype=jnp.float32)[:, None] * inv[None, :]
    return jnp.cos(ang).astype(dtype), jnp.sin(ang).astype(dtype)


def apply_rope(t, cos, sin):
    t1, t2 = jnp.split(t, 2, axis=-1)
    c = cos[None, :, None, None, :]
    s = sin[None, :, None, None, :]
    return jnp.concatenate([t1 * c - t2 * s, t1 * s + t2 * c], axis=-1)


def diff_attention(q, k, v, lam):
    B, S, H, _, Dh = q.shape
    nb = S // Q_BLOCK
    scale = Dh ** -0.5
    qb = q.reshape(B, nb, Q_BLOCK, H, 2, Dh).transpose(1, 0, 2, 3, 4, 5)

    def block(q_blk):
        s = jnp.einsum('bqhmd,bkhmd->bhmqk', q_blk, k,
                       preferred_element_type=jnp.float32) * scale
        p = jax.nn.softmax(s, axis=-1)
        p_diff = p[:, :, 0] - lam * p[:, :, 1]
        return jnp.einsum('bhqk,bkhe->bqhe', p_diff.astype(v.dtype), v)

    o = lax.map(block, qb)
    return o.transpose(1, 0, 2, 3, 4).reshape(B, S, H, v.shape[-1])


def multiscale_pool(u, pool_w, pool_scale):
    B, S, _ = u.shape
    ug = u.reshape(B, S, N_POOL_GROUPS, POOL_GROUP_DIM)
    cs = jnp.cumsum(ug.astype(jnp.float32), axis=1)
    cs = jnp.concatenate([jnp.zeros_like(cs[:, :1]), cs], axis=1)
    pos = jnp.arange(S)
    means = []
    for g, w in enumerate(POOL_WINDOWS):
        lo = jnp.clip(pos - w // 2, 0, S)
        hi = jnp.clip(pos + (w - w // 2), 0, S)
        cnt = (hi - lo).astype(jnp.float32)[None, :, None]
        means.append((cs[:, hi, g] - cs[:, lo, g]) / cnt)
    pooled = jnp.stack(means, axis=2).astype(u.dtype) - ug
    mixed = jnp.einsum('bsgc,gcd->bsgd', pooled, pool_w)
    return mixed.reshape(B, S, POOL_WIDTH) * pool_scale


def hybrid_mixer(x, w_in, lambda_q1, lambda_k1, lambda_q2, lambda_k2, attn_subln_g,
                 pool_w, pool_scale, w_branch_att, w_branch_pool, w_out,
                 lambda_init, cos, sin):
    B, S, _ = x.shape
    h = x @ w_in
    q, k, v, u, gate_logits = jnp.split(
        h, [QK_WIDTH, 2 * QK_WIDTH, 2 * QK_WIDTH + ATT_WIDTH,
            2 * QK_WIDTH + ATT_WIDTH + POOL_WIDTH], axis=-1)

    q = apply_rope(q.reshape(B, S, N_ATT_HEADS, 2, HEAD_DIM), cos, sin)
    k = apply_rope(k.reshape(B, S, N_ATT_HEADS, 2, HEAD_DIM), cos, sin)
    v = v.reshape(B, S, N_ATT_HEADS, V_HEAD_DIM)
    lam = (jnp.exp(jnp.sum(lambda_q1.astype(jnp.float32) * lambda_k1.astype(jnp.float32)))
           - jnp.exp(jnp.sum(lambda_q2.astype(jnp.float32) * lambda_k2.astype(jnp.float32)))
           + lambda_init)
    o = diff_attention(q, k, v, lam).astype(jnp.float32)
    o = o * lax.rsqrt(jnp.mean(jnp.square(o), axis=-1, keepdims=True) + RMS_EPS)
    o = (o * (1.0 - lambda_init)).astype(x.dtype) * attn_subln_g
    y_att = o.reshape(B, S, ATT_WIDTH)

    y_pool = multiscale_pool(u, pool_w, pool_scale)

    g_att, g_pool = jnp.split(jax.nn.sigmoid(gate_logits), N_BRANCHES, axis=-1)
    merged = g_att * (y_att @ w_branch_att) + g_pool * (y_pool @ w_branch_pool)
    return merged @ w_out


def setup_inputs(seed: int = 0) -> dict:
    key = jax.random.key(seed)
    ks = jax.random.split(key, 24)
    f32 = jnp.float32
    L = DEPTH

    def dense(k, shape, fan_in, scale=1.0):
        return jax.random.normal(k, shape, f32) * (fan_in ** -0.5) * scale

    def gain(k, shape):
        return 1.0 + 0.05 * jax.random.normal(k, shape, f32)

    def bias(k, shape):
        return 0.02 * jax.random.normal(k, shape, f32)

    return {
        "x": jax.random.normal(ks[0], (BATCH, SEQ, D_MODEL), f32),
        "ln1_g": gain(ks[1], (L, D_MODEL)),
        "ln1_b": bias(ks[2], (L, D_MODEL)),
        "ffn1_w_gate": dense(ks[3], (L, D_MODEL, D_FF), D_MODEL),
        "ffn1_w_up": dense(ks[4], (L, D_MODEL, D_FF), D_MODEL),
        "ffn1_w_down": dense(ks[5], (L, D_FF, D_MODEL), D_FF, DEEPNORM_BETA),
        "w_in": dense(ks[6], (L, D_MODEL, IN_WIDTH), D_MODEL),
        "lambda_q1": 0.1 * jax.random.normal(ks[7], (L, HEAD_DIM), f32),
        "lambda_k1": 0.1 * jax.random.normal(ks[8], (L, HEAD_DIM), f32),
        "lambda_q2": 0.1 * jax.random.normal(ks[9], (L, HEAD_DIM), f32),
        "lambda_k2": 0.1 * jax.random.normal(ks[10], (L, HEAD_DIM), f32),
        "attn_subln_g": gain(ks[11], (L, V_HEAD_DIM)),
        "pool_w": dense(ks[12], (L, N_POOL_GROUPS, POOL_GROUP_DIM, POOL_GROUP_DIM), POOL_GROUP_DIM),
        "pool_scale": gain(ks[13], (L, POOL_WIDTH)),
        "w_branch_att": dense(ks[14], (L, ATT_WIDTH, D_MODEL), ATT_WIDTH),
        "w_branch_pool": dense(ks[15], (L, POOL_WIDTH, D_MODEL), POOL_WIDTH),
        "w_out": dense(ks[16], (L, D_MODEL, D_MODEL), D_MODEL, DEEPNORM_BETA),
        "ln2_g": gain(ks[17], (L, D_MODEL)),
        "ln2_b": bias(ks[18], (L, D_MODEL)),
        "ffn2_w_gate": dense(ks[19], (L, D_MODEL, D_FF), D_MODEL),
        "ffn2_w_up": dense(ks[20], (L, D_MODEL, D_FF), D_MODEL),
        "ffn2_w_down": dense(ks[21], (L, D_FF, D_MODEL), D_FF, DEEPNORM_BETA),
        "ln3_g": gain(ks[22], (L, D_MODEL)),
        "ln3_b": bias(ks[23], (L, D_MODEL)),
    }


def reference(x, ln1_g, ln1_b, ffn1_w_gate, ffn1_w_up, ffn1_w_down, w_in,
              lambda_q1, lambda_k1, lambda_q2, lambda_k2, attn_subln_g,
              pool_w, pool_scale, w_branch_att, w_branch_pool, w_out,
              ln2_g, ln2_b, ffn2_w_gate, ffn2_w_up, ffn2_w_down, ln3_g, ln3_b):
    cos, sin = rope_tables(x.shape[1], x.dtype)
    for l in range(DEPTH):
        x = layer_norm(DEEPNORM_ALPHA * x
                       + 0.5 * swiglu(x, ffn1_w_gate[l], ffn1_w_up[l], ffn1_w_down[l]),
                       ln1_g[l], ln1_b[l])
        mix = hybrid_mixer(x, w_in[l], lambda_q1[l], lambda_k1[l], lambda_q2[l], lambda_k2[l],
                           attn_subln_g[l], pool_w[l], pool_scale[l], w_branch_att[l],
                           w_branch_pool[l], w_out[l], lambda_init_for(l), cos, sin)
        x = layer_norm(DEEPNORM_ALPHA * x + mix, ln2_g[l], ln2_b[l])
        x = layer_norm(DEEPNORM_ALPHA * x
                       + 0.5 * swiglu(x, ffn2_w_gate[l], ffn2_w_up[l], ffn2_w_down[l]),
                       ln3_g[l], ln3_b[l])
    return x
```

```python
import functools
import math

import jax
import jax.numpy as jnp
import numpy as np
from jax import lax
from jax.experimental import pallas as pl
from jax.experimental.pallas import tpu as pltpu

D_MODEL = 1024
DEPTH = 1
N_ATT_HEADS = 4
HEAD_DIM = 64
HALF_DIM = HEAD_DIM // 2
V_HEAD_DIM = 2 * HEAD_DIM
QK_WIDTH = N_ATT_HEADS * 2 * HEAD_DIM
ATT_WIDTH = N_ATT_HEADS * V_HEAD_DIM
POOL_WIDTH = D_MODEL // 2
POOL_WINDOWS = (2, 4, 8, 16)
N_POOL_GROUPS = len(POOL_WINDOWS)
POOL_GROUP_DIM = POOL_WIDTH // N_POOL_GROUPS
D_FF = 2816
ROPE_THETA = 10000.0
LN_EPS = 1e-5
RMS_EPS = 1e-5
DEEPNORM_ALPHA = (2.0 * DEPTH) ** 0.25
QK_SCALE = HEAD_DIM ** -0.5

LANES = 128
PAIR_WIDTH = 2 * LANES

FFN_ROWS = 512
FFN_CHUNK = 256
PROJ_ROWS = 512
MERGE_ROWS = 512
ATT_Q_ROWS = 256
VMEM_LIMIT_BYTES = 56 * 1024 * 1024

F32 = jnp.float32
BF16 = jnp.bfloat16


def _lambda_init(layer_idx):
    return 0.8 - 0.6 * math.exp(-0.3 * layer_idx)


def _layer_norm(y, g, b):
    mu = jnp.mean(y, axis=-1, keepdims=True)
    d = y - mu
    var = jnp.mean(d * d, axis=-1, keepdims=True)
    return d * lax.rsqrt(var + LN_EPS) * g + b


def _resident(shape):
    zeros = (0,) * len(shape)
    return pl.BlockSpec(shape, lambda *_: zeros, pipeline_mode=pl.Buffered(1))


def _params(n_axes):
    return pltpu.CompilerParams(
        dimension_semantics=("arbitrary",) * n_axes,
        vmem_limit_bytes=VMEM_LIMIT_BYTES,
    )


def _ffn_ln_kernel(x_ref, wg_ref, wu_ref, wd_ref, g_ref, b_ref, o_ref):
    x = x_ref[...]
    xb = x.astype(BF16)
    acc = None
    for c in range(D_FF // FFN_CHUNK):
        cols = slice(c * FFN_CHUNK, (c + 1) * FFN_CHUNK)
        gate = jnp.dot(xb, wg_ref[:, cols], preferred_element_type=F32)
        up = jnp.dot(xb, wu_ref[:, cols], preferred_element_type=F32)
        h = (gate * jax.nn.sigmoid(gate) * up).astype(BF16)
        part = jnp.dot(h, wd_ref[cols, :], preferred_element_type=F32)
        acc = part if acc is None else acc + part
    y = DEEPNORM_ALPHA * x + 0.5 * acc
    o_ref[...] = _layer_norm(y, g_ref[...], b_ref[...])


def _ffn_ln(x2d, w_gate, w_up, w_down, ln_g, ln_b):
    n_tok = x2d.shape[0]
    row_spec = pl.BlockSpec((FFN_ROWS, D_MODEL), lambda i: (i, 0))
    return pl.pallas_call(
        _ffn_ln_kernel,
        grid=(n_tok // FFN_ROWS,),
        in_specs=[
            row_spec,
            _resident((D_MODEL, D_FF)),
            _resident((D_MODEL, D_FF)),
            _resident((D_FF, D_MODEL)),
            _resident((1, D_MODEL)),
            _resident((1, D_MODEL)),
        ],
        out_specs=row_spec,
        out_shape=jax.ShapeDtypeStruct((n_tok, D_MODEL), F32),
        compiler_params=_params(1),
        name="ffn_ln",
    )(x2d, w_gate, w_up, w_down, ln_g, ln_b)


def _rotate_pairs(first, second, cos, sin):
    r1 = first * cos - second * sin
    r2 = first * sin + second * cos
    return [r1[:, :LANES], r2[:, :LANES], r1[:, LANES:], r2[:, LANES:]]


def _in_proj_kernel(x_ref, w_ref, cos_ref, sin_ref, q_ref, k_ref, v_ref, u_ref):
    xb = x_ref[...].astype(BF16)
    h = jnp.dot(xb, w_ref[...], preferred_element_type=F32)
    cos = jnp.concatenate([cos_ref[...], cos_ref[...]], axis=1)
    sin = jnp.concatenate([sin_ref[...], sin_ref[...]], axis=1)
    half = QK_WIDTH // 2
    for out_ref, base in ((q_ref, 0), (k_ref, QK_WIDTH)):
        blocks = _rotate_pairs(h[:, base:base + half], h[:, base + half:base + QK_WIDTH], cos, sin)
        for j, blk in enumerate(blocks):
            out_ref[:, j * LANES:(j + 1) * LANES] = blk.astype(BF16)
    v_ref[...] = h[:, 2 * QK_WIDTH:2 * QK_WIDTH + ATT_WIDTH].astype(BF16)
    u_ref[...] = h[:, 2 * QK_WIDTH + ATT_WIDTH:]


def _in_proj(x2d, w_proj, cos_t, sin_t, seq):
    n_tok = x2d.shape[0]
    width = w_proj.shape[1]
    tiles_per_seq = seq // PROJ_ROWS
    row = lambda w: pl.BlockSpec((PROJ_ROWS, w), lambda i: (i, 0))
    table = pl.BlockSpec((PROJ_ROWS, LANES), lambda i: (i % tiles_per_seq, 0))
    return pl.pallas_call(
        _in_proj_kernel,
        grid=(n_tok // PROJ_ROWS,),
        in_specs=[row(D_MODEL), _resident((D_MODEL, width)), table, table],
        out_specs=[row(QK_WIDTH), row(QK_WIDTH), row(ATT_WIDTH), row(POOL_WIDTH)],
        out_shape=[
            jax.ShapeDtypeStruct((n_tok, QK_WIDTH), BF16),
            jax.ShapeDtypeStruct((n_tok, QK_WIDTH), BF16),
            jax.ShapeDtypeStruct((n_tok, ATT_WIDTH), BF16),
            jax.ShapeDtypeStruct((n_tok, POOL_WIDTH), F32),
        ],
        compiler_params=_params(1),
        name="in_proj",
    )(x2d, w_proj, cos_t, sin_t)


def _pool_kernel(u_ref, pw_ref, ps_ref, o_ref):
    seq = u_ref.shape[0]
    pos = lax.broadcasted_iota(jnp.int32, (seq, POOL_GROUP_DIM), 0)

    def prev(a, k):
        return jnp.where(pos >= k, pltpu.roll(a, k, axis=0), 0.0)

    def nxt(a, k):
        return jnp.where(pos < seq - k, pltpu.roll(a, seq - k, axis=0), 0.0)

    for g, w in enumerate(POOL_WINDOWS):
        cols = slice(g * POOL_GROUP_DIM, (g + 1) * POOL_GROUP_DIM)
        u = u_ref[:, cols]
        n = w // 2
        back = prev(u, 1)
        fwd = u
        k = 1
        while k < n:
            back = back + prev(back, k)
            fwd = fwd + nxt(fwd, k)
            k *= 2
        cnt = jnp.minimum(pos + n, seq) - jnp.maximum(pos - n, 0)
        pooled = (back + fwd) / cnt.astype(F32) - u
        mixed = jnp.dot(pooled.astype(BF16), pw_ref[g], preferred_element_type=F32)
        o_ref[:, cols] = (mixed * ps_ref[:, cols]).astype(BF16)


def _pool(u2d, pool_w, pool_scale, batch, seq):
    blk = pl.BlockSpec((seq, POOL_WIDTH), lambda b: (b, 0))
    return pl.pallas_call(
        _pool_kernel,
        grid=(batch,),
        in_specs=[
            blk,
            _resident((N_POOL_GROUPS, POOL_GROUP_DIM, POOL_GROUP_DIM)),
            _resident((1, POOL_WIDTH)),
        ],
        out_specs=blk,
        out_shape=jax.ShapeDtypeStruct((batch * seq, POOL_WIDTH), BF16),
        compiler_params=_params(1),
        name="pool",
    )(u2d, pool_w, pool_scale)


def _attention_kernel(lam_ref, q_ref, k_ref, v_ref, g_ref, o_ref, *, lambda_init):
    lv = lam_ref[...]
    lam = (jnp.exp(jnp.sum(lv[0:1] * lv[1:2], axis=-1, keepdims=True))
           - jnp.exp(jnp.sum(lv[2:3] * lv[3:4], axis=-1, keepdims=True))
           + lambda_init)
    group = (lax.broadcasted_iota(jnp.int32, (1, PAIR_WIDTH), 1) % LANES) // HALF_DIM
    contract_last = (((1,), (1,)), ((), ()))
    for pair in range(N_ATT_HEADS // 2):
        cols = slice(pair * PAIR_WIDTH, (pair + 1) * PAIR_WIDTH)
        q_pair = q_ref[:, cols]
        k_pair = k_ref[:, cols]
        for sub in range(2):
            head = 2 * pair + sub
            probs = []
            for m in range(2):
                q_map = jnp.where(group == 2 * sub + m, q_pair, jnp.zeros_like(q_pair))
                s = lax.dot_general(q_map, k_pair, contract_last, preferred_element_type=F32)
                e = jnp.exp(s - jnp.max(s, axis=-1, keepdims=True))
                probs.append(e * (1.0 / jnp.sum(e, axis=-1, keepdims=True)))
            p_diff = (probs[0] - lam * probs[1]).astype(BF16)
            hcols = slice(head * V_HEAD_DIM, (head + 1) * V_HEAD_DIM)
            o = jnp.dot(p_diff, v_ref[:, hcols], preferred_element_type=F32)
            o = o * lax.rsqrt(jnp.mean(o * o, axis=-1, keepdims=True) + RMS_EPS)
            o_ref[:, hcols] = ((o * (1.0 - lambda_init)) * g_ref[...]).astype(BF16)


def _attention(lam_vecs, q2d, k2d, v2d, subln_g, batch, seq, lambda_init):
    q_tiles = seq // ATT_Q_ROWS
    q_spec = pl.BlockSpec((ATT_Q_ROWS, QK_WIDTH), lambda b, i: (b * q_tiles + i, 0))
    kv_spec = lambda w: pl.BlockSpec((seq, w), lambda b, i: (b, 0))
    return pl.pallas_call(
        functools.partial(_attention_kernel, lambda_init=lambda_init),
        grid=(batch, q_tiles),
        in_specs=[
            _resident((4, HEAD_DIM)),
            q_spec,
            kv_spec(QK_WIDTH),
            kv_spec(ATT_WIDTH),
            _resident((1, V_HEAD_DIM)),
        ],
        out_specs=pl.BlockSpec((ATT_Q_ROWS, ATT_WIDTH), lambda b, i: (b * q_tiles + i, 0)),
        out_shape=jax.ShapeDtypeStruct((batch * seq, ATT_WIDTH), BF16),
        compiler_params=_params(2),
        name="diff_attention",
    )(lam_vecs, q2d, k2d, v2d, subln_g)


def _merge_ln_kernel(x_ref, ya_ref, yp_ref, wgate_ref, wa_ref, wp_ref, wo_ref, g_ref, b_ref, o_ref):
    x = x_ref[...]
    xb = x.astype(BF16)
    gates = jax.nn.sigmoid(jnp.dot(xb, wgate_ref[...], preferred_element_type=F32))
    att = jnp.dot(ya_ref[...], wa_ref[...], preferred_element_type=F32)
    pool = jnp.dot(yp_ref[...], wp_ref[...], preferred_element_type=F32)
    merged = gates[:, :D_MODEL] * att + gates[:, D_MODEL:] * pool
    mix = jnp.dot(merged.astype(BF16), wo_ref[...], preferred_element_type=F32)
    o_ref[...] = _layer_norm(DEEPNORM_ALPHA * x + mix, g_ref[...], b_ref[...])


def _merge_ln(x2d, y_att, y_pool, w_gates, w_att, w_pool, w_out, ln_g, ln_b):
    n_tok = x2d.shape[0]
    row = lambda w: pl.BlockSpec((MERGE_ROWS, w), lambda i: (i, 0))
    return pl.pallas_call(
        _merge_ln_kernel,
        grid=(n_tok // MERGE_ROWS,),
        in_specs=[
            row(D_MODEL), row(ATT_WIDTH), row(POOL_WIDTH),
            _resident((D_MODEL, 2 * D_MODEL)),
            _resident((ATT_WIDTH, D_MODEL)),
            _resident((POOL_WIDTH, D_MODEL)),
            _resident((D_MODEL, D_MODEL)),
            _resident((1, D_MODEL)),
            _resident((1, D_MODEL)),
        ],
        out_specs=row(D_MODEL),
        out_shape=jax.ShapeDtypeStruct((n_tok, D_MODEL), F32),
        compiler_params=_params(1),
        name="merge_ln",
    )(x2d, y_att, y_pool, w_gates, w_att, w_pool, w_out, ln_g, ln_b)


def _rotary_column_order():
    first = [h * 2 * HEAD_DIM + m * HEAD_DIM + d
             for h in range(N_ATT_HEADS) for m in range(2) for d in range(HALF_DIM)]
    first = np.asarray(first, dtype=np.int32)
    return np.concatenate([first, first + HALF_DIM])


def _rope_tables(seq):
    inv = 1.0 / (ROPE_THETA ** (jnp.arange(0, HEAD_DIM, 2, dtype=F32) / HEAD_DIM))
    ang = jnp.arange(seq, dtype=F32)[:, None] * inv[None, :]
    reps = LANES // HALF_DIM
    return jnp.tile(jnp.cos(ang), (1, reps)), jnp.tile(jnp.sin(ang), (1, reps))


def _row(v):
    return v.reshape(1, -1)


def kernel(x, ln1_g, ln1_b, ffn1_w_gate, ffn1_w_up, ffn1_w_down, w_in, lambda_q1, lambda_k1, lambda_q2, lambda_k2, attn_subln_g, pool_w, pool_scale, w_branch_att, w_branch_pool, w_out, ln2_g, ln2_b, ffn2_w_gate, ffn2_w_up, ffn2_w_down, ln3_g, ln3_b):
    batch, seq, _ = x.shape
    order = _rotary_column_order()
    cos_t, sin_t = _rope_tables(seq)
    h = x.reshape(batch * seq, D_MODEL)
    for l in range(DEPTH):
        w = w_in[l]
        w_proj = jnp.concatenate(
            [w[:, :QK_WIDTH][:, order] * QK_SCALE,
             w[:, QK_WIDTH:2 * QK_WIDTH][:, order],
             w[:, 2 * QK_WIDTH:2 * QK_WIDTH + ATT_WIDTH + POOL_WIDTH]], axis=1).astype(BF16)
        w_gates = w[:, 2 * QK_WIDTH + ATT_WIDTH + POOL_WIDTH:].astype(BF16)
        lam_vecs = jnp.stack([lambda_q1[l], lambda_k1[l], lambda_q2[l], lambda_k2[l]]).astype(F32)

        h = _ffn_ln(h, ffn1_w_gate[l].astype(BF16), ffn1_w_up[l].astype(BF16),
                    ffn1_w_down[l].astype(BF16), _row(ln1_g[l]), _row(ln1_b[l]))
        q, k, v, u = _in_proj(h, w_proj, cos_t, sin_t, seq)
        y_pool = _pool(u, pool_w[l].astype(BF16), _row(pool_scale[l]), batch, seq)
        y_att = _attention(lam_vecs, q, k, v, _row(attn_subln_g[l]), batch, seq, _lambda_init(l))
        h = _merge_ln(h, y_att, y_pool, w_gates, w_branch_att[l].astype(BF16),
                      w_branch_pool[l].astype(BF16), w_out[l].astype(BF16),
                      _row(ln2_g[l]), _row(ln2_b[l]))
        h = _ffn_ln(h, ffn2_w_gate[l].astype(BF16), ffn2_w_up[l].astype(BF16),
                    ffn2_w_down[l].astype(BF16), _row(ln3_g[l]), _row(ln3_b[l]))
    return h.reshape(batch, seq, D_MODEL)
```

```python
import functools
import math

import jax
import jax.numpy as jnp
import numpy as np
from jax import lax
from jax.experimental import pallas as pl
from jax.experimental.pallas import tpu as pltpu

D_MODEL = 1024
DEPTH = 1
N_ATT_HEADS = 4
HEAD_DIM = 64
HALF_DIM = HEAD_DIM // 2
V_HEAD_DIM = 2 * HEAD_DIM
QK_WIDTH = N_ATT_HEADS * 2 * HEAD_DIM
ATT_WIDTH = N_ATT_HEADS * V_HEAD_DIM
POOL_WIDTH = D_MODEL // 2
POOL_WINDOWS = (2, 4, 8, 16)
N_POOL_GROUPS = len(POOL_WINDOWS)
POOL_GROUP_DIM = POOL_WIDTH // N_POOL_GROUPS
D_FF = 2816
ROPE_THETA = 10000.0
LN_EPS = 1e-5
RMS_EPS = 1e-5
DEEPNORM_ALPHA = (2.0 * DEPTH) ** 0.25
QK_SCALE = HEAD_DIM ** -0.5 * math.log2(math.e)

LANES = 128
PAIR_WIDTH = 2 * LANES

FFN_ROWS = 512
FFN_CHUNK = 256
PROJ_ROWS = 512
MERGE_ROWS = 512
ATT_Q_ROWS = 512
VMEM_LIMIT_BYTES = 56 * 1024 * 1024

F32 = jnp.float32
BF16 = jnp.bfloat16


def _lambda_init(layer_idx):
    return 0.8 - 0.6 * math.exp(-0.3 * layer_idx)


def _layer_norm(y, g, b):
    mu = jnp.mean(y, axis=-1, keepdims=True)
    d = y - mu
    var = jnp.mean(d * d, axis=-1, keepdims=True)
    return d * lax.rsqrt(var + LN_EPS) * g + b


def _resident(shape):
    zeros = (0,) * len(shape)
    return pl.BlockSpec(shape, lambda *_: zeros, pipeline_mode=pl.Buffered(1))


def _params(n_axes):
    return pltpu.CompilerParams(
        dimension_semantics=("arbitrary",) * n_axes,
        vmem_limit_bytes=VMEM_LIMIT_BYTES,
    )


def _ffn_ln_kernel(x_ref, wg_ref, wu_ref, wd_ref, g_ref, b_ref, o_ref):
    x = x_ref[...]
    xb = x.astype(BF16)
    acc = None
    for c in range(D_FF // FFN_CHUNK):
        cols = slice(c * FFN_CHUNK, (c + 1) * FFN_CHUNK)
        gate = jnp.dot(xb, wg_ref[:, cols], preferred_element_type=F32)
        up = jnp.dot(xb, wu_ref[:, cols], preferred_element_type=F32)
        h = (gate * jax.nn.sigmoid(gate) * up).astype(BF16)
        part = jnp.dot(h, wd_ref[cols, :], preferred_element_type=F32)
        acc = part if acc is None else acc + part
    y = DEEPNORM_ALPHA * x + 0.5 * acc
    o_ref[...] = _layer_norm(y, g_ref[...], b_ref[...])


def _ffn_ln(x2d, w_gate, w_up, w_down, ln_g, ln_b):
    n_tok = x2d.shape[0]
    row_spec = pl.BlockSpec((FFN_ROWS, D_MODEL), lambda i: (i, 0))
    return pl.pallas_call(
        _ffn_ln_kernel,
        grid=(n_tok // FFN_ROWS,),
        in_specs=[
            row_spec,
            _resident((D_MODEL, D_FF)),
            _resident((D_MODEL, D_FF)),
            _resident((D_FF, D_MODEL)),
            _resident((1, D_MODEL)),
            _resident((1, D_MODEL)),
        ],
        out_specs=row_spec,
        out_shape=jax.ShapeDtypeStruct((n_tok, D_MODEL), F32),
        compiler_params=_params(1),
        name="ffn_ln",
    )(x2d, w_gate, w_up, w_down, ln_g, ln_b)


def _rotate_pairs(first, second, cos, sin):
    r1 = first * cos - second * sin
    r2 = first * sin + second * cos
    return [r1[:, :LANES], r2[:, :LANES], r1[:, LANES:], r2[:, LANES:]]


def _in_proj_kernel(x_ref, w_ref, cos_ref, sin_ref, q_ref, k_ref, v_ref, u_ref):
    xb = x_ref[...].astype(BF16)
    h = jnp.dot(xb, w_ref[...], preferred_element_type=F32)
    cos = jnp.concatenate([cos_ref[...], cos_ref[...]], axis=1)
    sin = jnp.concatenate([sin_ref[...], sin_ref[...]], axis=1)
    half = QK_WIDTH // 2
    for out_ref, base in ((q_ref, 0), (k_ref, QK_WIDTH)):
        blocks = _rotate_pairs(h[:, base:base + half], h[:, base + half:base + QK_WIDTH], cos, sin)
        for j, blk in enumerate(blocks):
            out_ref[:, j * LANES:(j + 1) * LANES] = blk.astype(BF16)
    v_ref[...] = h[:, 2 * QK_WIDTH:2 * QK_WIDTH + ATT_WIDTH].astype(BF16)
    u_ref[...] = h[:, 2 * QK_WIDTH + ATT_WIDTH:]


def _in_proj(x2d, w_proj, cos_t, sin_t, seq):
    n_tok = x2d.shape[0]
    width = w_proj.shape[1]
    tiles_per_seq = seq // PROJ_ROWS
    row = lambda w: pl.BlockSpec((PROJ_ROWS, w), lambda i: (i, 0))
    table = pl.BlockSpec((PROJ_ROWS, LANES), lambda i: (i % tiles_per_seq, 0))
    return pl.pallas_call(
        _in_proj_kernel,
        grid=(n_tok // PROJ_ROWS,),
        in_specs=[row(D_MODEL), _resident((D_MODEL, width)), table, table],
        out_specs=[row(QK_WIDTH), row(QK_WIDTH), row(ATT_WIDTH), row(POOL_WIDTH)],
        out_shape=[
            jax.ShapeDtypeStruct((n_tok, QK_WIDTH), BF16),
            jax.ShapeDtypeStruct((n_tok, QK_WIDTH), BF16),
            jax.ShapeDtypeStruct((n_tok, ATT_WIDTH), BF16),
            jax.ShapeDtypeStruct((n_tok, POOL_WIDTH), F32),
        ],
        compiler_params=_params(1),
        name="in_proj",
    )(x2d, w_proj, cos_t, sin_t)


def _pool_kernel(u_ref, pw_ref, ps_ref, o_ref):
    seq = u_ref.shape[0]
    pos = lax.broadcasted_iota(jnp.int32, (seq, POOL_GROUP_DIM), 0)

    def prev(a, k):
        return jnp.where(pos >= k, pltpu.roll(a, k, axis=0), 0.0)

    def nxt(a, k):
        return jnp.where(pos < seq - k, pltpu.roll(a, seq - k, axis=0), 0.0)

    for g, w in enumerate(POOL_WINDOWS):
        cols = slice(g * POOL_GROUP_DIM, (g + 1) * POOL_GROUP_DIM)
        u = u_ref[:, cols]
        n = w // 2
        back = prev(u, 1)
        fwd = u
        k = 1
        while k < n:
            back = back + prev(back, k)
            fwd = fwd + nxt(fwd, k)
            k *= 2
        cnt = jnp.minimum(pos + n, seq) - jnp.maximum(pos - n, 0)
        pooled = (back + fwd) / cnt.astype(F32) - u
        mixed = jnp.dot(pooled.astype(BF16), pw_ref[g], preferred_element_type=F32)
        o_ref[:, cols] = (mixed * ps_ref[:, cols]).astype(BF16)


def _pool(u2d, pool_w, pool_scale, batch, seq):
    blk = pl.BlockSpec((seq, POOL_WIDTH), lambda b: (b, 0))
    return pl.pallas_call(
        _pool_kernel,
        grid=(batch,),
        in_specs=[
            blk,
            _resident((N_POOL_GROUPS, POOL_GROUP_DIM, POOL_GROUP_DIM)),
            _resident((1, POOL_WIDTH)),
        ],
        out_specs=blk,
        out_shape=jax.ShapeDtypeStruct((batch * seq, POOL_WIDTH), BF16),
        compiler_params=_params(1),
        name="pool",
    )(u2d, pool_w, pool_scale)


def _attention_kernel(lam_ref, q_ref, k_ref, v_ref, g_ref, o_ref, vext_ref, *, lambda_init):
    @pl.when(pl.program_id(1) == 0)
    def _():
        ones = jnp.ones((v_ref.shape[0], V_HEAD_DIM), BF16)
        for head in range(N_ATT_HEADS):
            base = head * 2 * V_HEAD_DIM
            vext_ref[:, base:base + V_HEAD_DIM] = v_ref[:, head * V_HEAD_DIM:(head + 1) * V_HEAD_DIM]
            vext_ref[:, base + V_HEAD_DIM:base + 2 * V_HEAD_DIM] = ones

    lv = lam_ref[...]
    lam = (jnp.exp(jnp.sum(lv[0:1] * lv[1:2], axis=-1, keepdims=True))
           - jnp.exp(jnp.sum(lv[2:3] * lv[3:4], axis=-1, keepdims=True))
           + lambda_init)
    group = (lax.broadcasted_iota(jnp.int32, (1, PAIR_WIDTH), 1) % LANES) // HALF_DIM
    contract_last = (((1,), (1,)), ((), ()))
    for pair in range(N_ATT_HEADS // 2):
        cols = slice(pair * PAIR_WIDTH, (pair + 1) * PAIR_WIDTH)
        q_pair = q_ref[:, cols]
        k_pair = k_ref[:, cols]
        for sub in range(2):
            head = 2 * pair + sub
            v_ext = vext_ref[:, head * 2 * V_HEAD_DIM:(head + 1) * 2 * V_HEAD_DIM]
            maps = []
            for m in range(2):
                q_map = jnp.where(group == 2 * sub + m, q_pair, jnp.zeros_like(q_pair))
                s = lax.dot_general(q_map, k_pair, contract_last, preferred_element_type=F32)
                e = jnp.exp2(s - jnp.max(s, axis=-1, keepdims=True)).astype(BF16)
                ev = jnp.dot(e, v_ext, preferred_element_type=F32)
                maps.append(ev[:, :V_HEAD_DIM] / ev[:, V_HEAD_DIM:])
            o = maps[0] - lam * maps[1]
            hcols = slice(head * V_HEAD_DIM, (head + 1) * V_HEAD_DIM)
            o = o * lax.rsqrt(jnp.mean(o * o, axis=-1, keepdims=True) + RMS_EPS)
            o_ref[:, hcols] = ((o * (1.0 - lambda_init)) * g_ref[...]).astype(BF16)


def _attention(lam_vecs, q2d, k2d, v2d, subln_g, batch, seq, lambda_init):
    q_tiles = seq // ATT_Q_ROWS
    q_spec = pl.BlockSpec((ATT_Q_ROWS, QK_WIDTH), lambda b, i: (b * q_tiles + i, 0))
    kv_spec = lambda w: pl.BlockSpec((seq, w), lambda b, i: (b, 0))
    return pl.pallas_call(
        functools.partial(_attention_kernel, lambda_init=lambda_init),
        grid=(batch, q_tiles),
        in_specs=[
            _resident((4, HEAD_DIM)),
            q_spec,
            kv_spec(QK_WIDTH),
            kv_spec(ATT_WIDTH),
            _resident((1, V_HEAD_DIM)),
        ],
        out_specs=pl.BlockSpec((ATT_Q_ROWS, ATT_WIDTH), lambda b, i: (b * q_tiles + i, 0)),
        out_shape=jax.ShapeDtypeStruct((batch * seq, ATT_WIDTH), BF16),
        scratch_shapes=[pltpu.VMEM((seq, 2 * ATT_WIDTH), BF16)],
        compiler_params=_params(2),
        name="diff_attention",
    )(lam_vecs, q2d, k2d, v2d, subln_g)


def _merge_ln_kernel(x_ref, ya_ref, yp_ref, wgate_ref, wa_ref, wp_ref, wo_ref, g_ref, b_ref, o_ref):
    x = x_ref[...]
    xb = x.astype(BF16)
    gates = jax.nn.sigmoid(jnp.dot(xb, wgate_ref[...], preferred_element_type=F32))
    att = jnp.dot(ya_ref[...], wa_ref[...], preferred_element_type=F32)
    pool = jnp.dot(yp_ref[...], wp_ref[...], preferred_element_type=F32)
    merged = gates[:, :D_MODEL] * att + gates[:, D_MODEL:] * pool
    mix = jnp.dot(merged.astype(BF16), wo_ref[...], preferred_element_type=F32)
    o_ref[...] = _layer_norm(DEEPNORM_ALPHA * x + mix, g_ref[...], b_ref[...])


def _merge_ln(x2d, y_att, y_pool, w_gates, w_att, w_pool, w_out, ln_g, ln_b):
    n_tok = x2d.shape[0]
    row = lambda w: pl.BlockSpec((MERGE_ROWS, w), lambda i: (i, 0))
    return pl.pallas_call(
        _merge_ln_kernel,
        grid=(n_tok // MERGE_ROWS,),
        in_specs=[
            row(D_MODEL), row(ATT_WIDTH), row(POOL_WIDTH),
            _resident((D_MODEL, 2 * D_MODEL)),
            _resident((ATT_WIDTH, D_MODEL)),
            _resident((POOL_WIDTH, D_MODEL)),
            _resident((D_MODEL, D_MODEL)),
            _resident((1, D_MODEL)),
            _resident((1, D_MODEL)),
        ],
        out_specs=row(D_MODEL),
        out_shape=jax.ShapeDtypeStruct((n_tok, D_MODEL), F32),
        compiler_params=_params(1),
        name="merge_ln",
    )(x2d, y_att, y_pool, w_gates, w_att, w_pool, w_out, ln_g, ln_b)


def _rotary_column_order():
    first = [h * 2 * HEAD_DIM + m * HEAD_DIM + d
             for h in range(N_ATT_HEADS) for m in range(2) for d in range(HALF_DIM)]
    first = np.asarray(first, dtype=np.int32)
    return np.concatenate([first, first + HALF_DIM])


def _rope_tables(seq):
    inv = 1.0 / (ROPE_THETA ** (jnp.arange(0, HEAD_DIM, 2, dtype=F32) / HEAD_DIM))
    ang = jnp.arange(seq, dtype=F32)[:, None] * inv[None, :]
    reps = LANES // HALF_DIM
    return jnp.tile(jnp.cos(ang), (1, reps)), jnp.tile(jnp.sin(ang), (1, reps))


def _row(v):
    return v.reshape(1, -1)


def kernel(x, ln1_g, ln1_b, ffn1_w_gate, ffn1_w_up, ffn1_w_down, w_in, lambda_q1, lambda_k1, lambda_q2, lambda_k2, attn_subln_g, pool_w, pool_scale, w_branch_att, w_branch_pool, w_out, ln2_g, ln2_b, ffn2_w_gate, ffn2_w_up, ffn2_w_down, ln3_g, ln3_b):
    batch, seq, _ = x.shape
    order = _rotary_column_order()
    cos_t, sin_t = _rope_tables(seq)
    h = x.reshape(batch * seq, D_MODEL)
    for l in range(DEPTH):
        w = w_in[l]
        w_proj = jnp.concatenate(
            [w[:, :QK_WIDTH][:, order] * QK_SCALE,
             w[:, QK_WIDTH:2 * QK_WIDTH][:, order],
             w[:, 2 * QK_WIDTH:2 * QK_WIDTH + ATT_WIDTH + POOL_WIDTH]], axis=1).astype(BF16)
        w_gates = w[:, 2 * QK_WIDTH + ATT_WIDTH + POOL_WIDTH:].astype(BF16)
        lam_vecs = jnp.stack([lambda_q1[l], lambda_k1[l], lambda_q2[l], lambda_k2[l]]).astype(F32)

        h = _ffn_ln(h, ffn1_w_gate[l].astype(BF16), ffn1_w_up[l].astype(BF16),
                    ffn1_w_down[l].astype(BF16), _row(ln1_g[l]), _row(ln1_b[l]))
        q, k, v, u = _in_proj(h, w_proj, cos_t, sin_t, seq)
        y_pool = _pool(u, pool_w[l].astype(BF16), _row(pool_scale[l]), batch, seq)
        y_att = _attention(lam_vecs, q, k, v, _row(attn_subln_g[l]), batch, seq, _lambda_init(l))
        h = _merge_ln(h, y_att, y_pool, w_gates, w_branch_att[l].astype(BF16),
                      w_branch_pool[l].astype(BF16), w_out[l].astype(BF16),
                      _row(ln2_g[l]), _row(ln2_b[l]))
        h = _ffn_ln(h, ffn2_w_gate[l].astype(BF16), ffn2_w_up[l].astype(BF16),
                    ffn2_w_down[l].astype(BF16), _row(ln3_g[l]), _row(ln3_b[l]))
    return h.reshape(batch, seq, D_MODEL)
```

```python
import functools
import math

import jax
import jax.numpy as jnp
import numpy as np
from jax import lax
from jax.experimental import pallas as pl
from jax.experimental.pallas import tpu as pltpu

D_MODEL = 1024
DEPTH = 1
N_ATT_HEADS = 4
HEAD_DIM = 64
HALF_DIM = HEAD_DIM // 2
V_HEAD_DIM = 2 * HEAD_DIM
QK_WIDTH = N_ATT_HEADS * 2 * HEAD_DIM
ATT_WIDTH = N_ATT_HEADS * V_HEAD_DIM
POOL_WIDTH = D_MODEL // 2
POOL_WINDOWS = (2, 4, 8, 16)
N_POOL_GROUPS = len(POOL_WINDOWS)
POOL_GROUP_DIM = POOL_WIDTH // N_POOL_GROUPS
D_FF = 2816
ROPE_THETA = 10000.0
LN_EPS = 1e-5
RMS_EPS = 1e-5
DEEPNORM_ALPHA = (2.0 * DEPTH) ** 0.25
QK_SCALE = HEAD_DIM ** -0.5 * math.log2(math.e)

LANES = 128
SUBLANES = 8
PAIR_WIDTH = 2 * LANES

FFN_ROWS = 512
FFN_CHUNK = 256
FFN_ANCHOR_CHUNK = 4
PROJ_ROWS = 512
MERGE_ROWS = 512
ATT_Q_ROWS = 1024
VMEM_LIMIT_BYTES = 56 * 1024 * 1024

F32 = jnp.float32
BF16 = jnp.bfloat16


def _lambda_init(layer_idx):
    return 0.8 - 0.6 * math.exp(-0.3 * layer_idx)


def _layer_norm(y, g, b):
    mu = jnp.mean(y, axis=-1, keepdims=True)
    d = y - mu
    var = jnp.mean(d * d, axis=-1, keepdims=True)
    return d * lax.rsqrt(var + LN_EPS) * g + b


def _resident(shape):
    zeros = (0,) * len(shape)
    return pl.BlockSpec(shape, lambda *_: zeros, pipeline_mode=pl.Buffered(1))


def _params(n_axes):
    return pltpu.CompilerParams(
        dimension_semantics=("arbitrary",) * n_axes,
        vmem_limit_bytes=VMEM_LIMIT_BYTES,
    )


def _zero_bits_after(values, width):
    bits = lax.bitcast_convert_type(values, jnp.uint32)
    while bits.shape[0] > SUBLANES:
        half = bits.shape[0] // 2
        bits = bits[:half] | bits[half:]
    while bits.shape[1] > LANES:
        half = bits.shape[1] // 2
        bits = bits[:, :half] | bits[:, half:]
    zero = ((bits >> 16) >> 16)[0:1]
    return jnp.concatenate([zero] * (width // LANES), axis=1)


def _select_after(x, other, zero_bits):
    return jnp.where(zero_bits == 0, x, other)


def _swiglu_residual(x, wg_ref, wu_ref, wd_ref, before_anchor=None):
    xb = x.astype(BF16)
    acc = None
    for c in range(D_FF // FFN_CHUNK):
        cols = slice(c * FFN_CHUNK, (c + 1) * FFN_CHUNK)
        gate = jnp.dot(xb, wg_ref[:, cols], preferred_element_type=F32)
        up = jnp.dot(xb, wu_ref[:, cols], preferred_element_type=F32)
        if before_anchor is not None and c == FFN_ANCHOR_CHUNK:
            up = _select_after(up, gate, _zero_bits_after(before_anchor, FFN_CHUNK))
        h = (gate * jax.nn.sigmoid(gate) * up).astype(BF16)
        part = jnp.dot(h, wd_ref[cols, :], preferred_element_type=F32)
        acc = part if acc is None else acc + part
    return DEEPNORM_ALPHA * x + 0.5 * acc


def _ffn_ln_kernel(x_ref, wg_ref, wu_ref, wd_ref, g_ref, b_ref, o_ref, y_ref):
    i = pl.program_id(0)
    last = pl.num_programs(0) - 1

    @pl.when(i == 0)
    def _():
        y_ref[...] = jnp.zeros_like(y_ref)

    @pl.when(i < last)
    def _():
        normed = _layer_norm(y_ref[...], g_ref[...], b_ref[...])
        o_ref[...] = normed
        y_ref[...] = _swiglu_residual(x_ref[...], wg_ref, wu_ref, wd_ref, before_anchor=normed)

    @pl.when(i == last)
    def _():
        o_ref[...] = _layer_norm(y_ref[...], g_ref[...], b_ref[...])


def _skewed_specs(n_tiles, rows):
    cur = lambda w: pl.BlockSpec((rows, w), lambda i: (jnp.minimum(i, n_tiles - 1), 0))
    prev = lambda w: pl.BlockSpec((rows, w), lambda i: (jnp.maximum(i - 1, 0), 0))
    return cur, prev


def _ffn_ln(x2d, w_gate, w_up, w_down, ln_g, ln_b):
    n_tok = x2d.shape[0]
    n_tiles = n_tok // FFN_ROWS
    cur, prev = _skewed_specs(n_tiles, FFN_ROWS)
    return pl.pallas_call(
        _ffn_ln_kernel,
        grid=(n_tiles + 1,),
        in_specs=[
            cur(D_MODEL),
            _resident((D_MODEL, D_FF)),
            _resident((D_MODEL, D_FF)),
            _resident((D_FF, D_MODEL)),
            _resident((1, D_MODEL)),
            _resident((1, D_MODEL)),
        ],
        out_specs=prev(D_MODEL),
        out_shape=jax.ShapeDtypeStruct((n_tok, D_MODEL), F32),
        scratch_shapes=[pltpu.VMEM((FFN_ROWS, D_MODEL), F32)],
        compiler_params=_params(1),
        name="ffn_ln",
    )(x2d, w_gate, w_up, w_down, ln_g, ln_b)


def _rotate_pairs(first, second, cos, sin):
    r1 = first * cos - second * sin
    r2 = first * sin + second * cos
    return [r1[:, :LANES], r2[:, :LANES], r1[:, LANES:], r2[:, LANES:]]


def _in_proj_kernel(x_ref, w_ref, cos_ref, sin_ref, q_ref, k_ref, v_ref, u_ref):
    xb = x_ref[...].astype(BF16)
    h = jnp.dot(xb, w_ref[...], preferred_element_type=F32)
    cos = jnp.concatenate([cos_ref[...], cos_ref[...]], axis=1)
    sin = jnp.concatenate([sin_ref[...], sin_ref[...]], axis=1)
    half = QK_WIDTH // 2
    for out_ref, base in ((q_ref, 0), (k_ref, QK_WIDTH)):
        blocks = _rotate_pairs(h[:, base:base + half], h[:, base + half:base + QK_WIDTH], cos, sin)
        for j, blk in enumerate(blocks):
            out_ref[:, j * LANES:(j + 1) * LANES] = blk.astype(BF16)
    v_ref[...] = h[:, 2 * QK_WIDTH:2 * QK_WIDTH + ATT_WIDTH].astype(BF16)
    u_ref[...] = h[:, 2 * QK_WIDTH + ATT_WIDTH:]


def _in_proj(x2d, w_proj, cos_t, sin_t, seq):
    n_tok = x2d.shape[0]
    width = w_proj.shape[1]
    tiles_per_seq = seq // PROJ_ROWS
    row = lambda w: pl.BlockSpec((PROJ_ROWS, w), lambda i: (i, 0))
    table = pl.BlockSpec((PROJ_ROWS, LANES), lambda i: (i % tiles_per_seq, 0))
    return pl.pallas_call(
        _in_proj_kernel,
        grid=(n_tok // PROJ_ROWS,),
        in_specs=[row(D_MODEL), _resident((D_MODEL, width)), table, table],
        out_specs=[row(QK_WIDTH), row(QK_WIDTH), row(ATT_WIDTH), row(POOL_WIDTH)],
        out_shape=[
            jax.ShapeDtypeStruct((n_tok, QK_WIDTH), BF16),
            jax.ShapeDtypeStruct((n_tok, QK_WIDTH), BF16),
            jax.ShapeDtypeStruct((n_tok, ATT_WIDTH), BF16),
            jax.ShapeDtypeStruct((n_tok, POOL_WIDTH), F32),
        ],
        compiler_params=_params(1),
        name="in_proj",
    )(x2d, w_proj, cos_t, sin_t)


def _pool_kernel(u_ref, pw_ref, ps_ref, o_ref):
    seq = u_ref.shape[0]
    pos = lax.broadcasted_iota(jnp.int32, (seq, POOL_GROUP_DIM), 0)

    def prev(a, k):
        return jnp.where(pos >= k, pltpu.roll(a, k, axis=0), 0.0)

    def nxt(a, k):
        return jnp.where(pos < seq - k, pltpu.roll(a, seq - k, axis=0), 0.0)

    for g, w in enumerate(POOL_WINDOWS):
        cols = slice(g * POOL_GROUP_DIM, (g + 1) * POOL_GROUP_DIM)
        u = u_ref[:, cols]
        n = w // 2
        back = prev(u, 1)
        fwd = u
        k = 1
        while k < n:
            back = back + prev(back, k)
            fwd = fwd + nxt(fwd, k)
            k *= 2
        cnt = jnp.minimum(pos + n, seq) - jnp.maximum(pos - n, 0)
        pooled = (back + fwd) / cnt.astype(F32) - u
        mixed = jnp.dot(pooled.astype(BF16), pw_ref[g], preferred_element_type=F32)
        o_ref[:, cols] = (mixed * ps_ref[:, cols]).astype(BF16)


def _pool(u2d, pool_w, pool_scale, batch, seq):
    blk = pl.BlockSpec((seq, POOL_WIDTH), lambda b: (b, 0))
    return pl.pallas_call(
        _pool_kernel,
        grid=(batch,),
        in_specs=[
            blk,
            _resident((N_POOL_GROUPS, POOL_GROUP_DIM, POOL_GROUP_DIM)),
            _resident((1, POOL_WIDTH)),
        ],
        out_specs=blk,
        out_shape=jax.ShapeDtypeStruct((batch * seq, POOL_WIDTH), BF16),
        compiler_params=_params(1),
        name="pool",
    )(u2d, pool_w, pool_scale)


def _attention_kernel(lam_ref, q_ref, k_ref, v_ref, g_ref, o_ref, vext_ref, *, lambda_init):
    @pl.when(pl.program_id(1) == 0)
    def _():
        ones = jnp.ones((v_ref.shape[0], V_HEAD_DIM), BF16)
        for head in range(N_ATT_HEADS):
            base = head * 2 * V_HEAD_DIM
            vext_ref[:, base:base + V_HEAD_DIM] = v_ref[:, head * V_HEAD_DIM:(head + 1) * V_HEAD_DIM]
            vext_ref[:, base + V_HEAD_DIM:base + 2 * V_HEAD_DIM] = ones

    lv = lam_ref[...]
    lam = (jnp.exp(jnp.sum(lv[0:1] * lv[1:2], axis=-1, keepdims=True))
           - jnp.exp(jnp.sum(lv[2:3] * lv[3:4], axis=-1, keepdims=True))
           + lambda_init)
    group = (lax.broadcasted_iota(jnp.int32, (1, PAIR_WIDTH), 1) % LANES) // HALF_DIM
    contract_last = (((1,), (1,)), ((), ()))
    for pair in range(N_ATT_HEADS // 2):
        cols = slice(pair * PAIR_WIDTH, (pair + 1) * PAIR_WIDTH)
        q_pair = q_ref[:, cols]
        k_pair = k_ref[:, cols]
        for sub in range(2):
            head = 2 * pair + sub
            v_ext = vext_ref[:, head * 2 * V_HEAD_DIM:(head + 1) * 2 * V_HEAD_DIM]
            maps = []
            for m in range(2):
                q_map = jnp.where(group == 2 * sub + m, q_pair, jnp.zeros_like(q_pair))
                s = lax.dot_general(q_map, k_pair, contract_last, preferred_element_type=F32)
                e = jnp.exp2(s - jnp.max(s, axis=-1, keepdims=True)).astype(BF16)
                ev = jnp.dot(e, v_ext, preferred_element_type=F32)
                maps.append(ev[:, :V_HEAD_DIM] / ev[:, V_HEAD_DIM:])
            o = maps[0] - lam * maps[1]
            hcols = slice(head * V_HEAD_DIM, (head + 1) * V_HEAD_DIM)
            o = o * lax.rsqrt(jnp.mean(o * o, axis=-1, keepdims=True) + RMS_EPS)
            o_ref[:, hcols] = ((o * (1.0 - lambda_init)) * g_ref[...]).astype(BF16)


def _attention(lam_vecs, q2d, k2d, v2d, subln_g, batch, seq, lambda_init):
    q_tiles = seq // ATT_Q_ROWS
    q_spec = pl.BlockSpec((ATT_Q_ROWS, QK_WIDTH), lambda b, i: (b * q_tiles + i, 0))
    kv_spec = lambda w: pl.BlockSpec((seq, w), lambda b, i: (b, 0))
    return pl.pallas_call(
        functools.partial(_attention_kernel, lambda_init=lambda_init),
        grid=(batch, q_tiles),
        in_specs=[
            _resident((4, HEAD_DIM)),
            q_spec,
            kv_spec(QK_WIDTH),
            kv_spec(ATT_WIDTH),
            _resident((1, V_HEAD_DIM)),
        ],
        out_specs=pl.BlockSpec((ATT_Q_ROWS, ATT_WIDTH), lambda b, i: (b * q_tiles + i, 0)),
        out_shape=jax.ShapeDtypeStruct((batch * seq, ATT_WIDTH), BF16),
        scratch_shapes=[pltpu.VMEM((seq, 2 * ATT_WIDTH), BF16)],
        compiler_params=_params(2),
        name="diff_attention",
    )(lam_vecs, q2d, k2d, v2d, subln_g)


def _merge_ln_kernel(x_ref, ya_ref, yp_ref, wgate_ref, wa_ref, wp_ref, wo_ref, g_ref, b_ref, o_ref, y_ref):
    i = pl.program_id(0)
    last = pl.num_programs(0) - 1

    @pl.when(i == 0)
    def _():
        y_ref[...] = jnp.zeros_like(y_ref)

    @pl.when(i < last)
    def _():
        normed = _layer_norm(y_ref[...], g_ref[...], b_ref[...])
        o_ref[...] = normed
        x = x_ref[...]
        xb = x.astype(BF16)
        gates = jax.nn.sigmoid(jnp.dot(xb, wgate_ref[...], preferred_element_type=F32))
        att = jnp.dot(ya_ref[...], wa_ref[...], preferred_element_type=F32)
        pool = jnp.dot(yp_ref[...], wp_ref[...], preferred_element_type=F32)
        anchored = _select_after(pool[:, :PAIR_WIDTH], att[:, :PAIR_WIDTH], _zero_bits_after(normed, PAIR_WIDTH))
        pool = jnp.concatenate([anchored, pool[:, PAIR_WIDTH:]], axis=1)
        merged = gates[:, :D_MODEL] * att + gates[:, D_MODEL:] * pool
        mix = jnp.dot(merged.astype(BF16), wo_ref[...], preferred_element_type=F32)
        y_ref[...] = DEEPNORM_ALPHA * x + mix

    @pl.when(i == last)
    def _():
        o_ref[...] = _layer_norm(y_ref[...], g_ref[...], b_ref[...])


def _merge_ln(x2d, y_att, y_pool, w_gates, w_att, w_pool, w_out, ln_g, ln_b):
    n_tok = x2d.shape[0]
    n_tiles = n_tok // MERGE_ROWS
    cur, prev = _skewed_specs(n_tiles, MERGE_ROWS)
    return pl.pallas_call(
        _merge_ln_kernel,
        grid=(n_tiles + 1,),
        in_specs=[
            cur(D_MODEL), cur(ATT_WIDTH), cur(POOL_WIDTH),
            _resident((D_MODEL, 2 * D_MODEL)),
            _resident((ATT_WIDTH, D_MODEL)),
            _resident((POOL_WIDTH, D_MODEL)),
            _resident((D_MODEL, D_MODEL)),
            _resident((1, D_MODEL)),
            _resident((1, D_MODEL)),
        ],
        out_specs=prev(D_MODEL),
        out_shape=jax.ShapeDtypeStruct((n_tok, D_MODEL), F32),
        scratch_shapes=[pltpu.VMEM((MERGE_ROWS, D_MODEL), F32)],
        compiler_params=_params(1),
        name="merge_ln",
    )(x2d, y_att, y_pool, w_gates, w_att, w_pool, w_out, ln_g, ln_b)


def _rotary_column_order():
    first = [h * 2 * HEAD_DIM + m * HEAD_DIM + d
             for h in range(N_ATT_HEADS) for m in range(2) for d in range(HALF_DIM)]
    first = np.asarray(first, dtype=np.int32)
    return np.concatenate([first, first + HALF_DIM])


def _rope_tables(seq):
    inv = 1.0 / (ROPE_THETA ** (jnp.arange(0, HEAD_DIM, 2, dtype=F32) / HEAD_DIM))
    ang = jnp.arange(seq, dtype=F32)[:, None] * inv[None, :]
    reps = LANES // HALF_DIM
    return jnp.tile(jnp.cos(ang), (1, reps)), jnp.tile(jnp.sin(ang), (1, reps))


def _row(v):
    return v.reshape(1, -1)


def kernel(x, ln1_g, ln1_b, ffn1_w_gate, ffn1_w_up, ffn1_w_down, w_in, lambda_q1, lambda_k1, lambda_q2, lambda_k2, attn_subln_g, pool_w, pool_scale, w_branch_att, w_branch_pool, w_out, ln2_g, ln2_b, ffn2_w_gate, ffn2_w_up, ffn2_w_down, ln3_g, ln3_b):
    batch, seq, _ = x.shape
    order = _rotary_column_order()
    cos_t, sin_t = _rope_tables(seq)
    h = x.reshape(batch * seq, D_MODEL)
    for l in range(DEPTH):
        w = w_in[l]
        w_proj = jnp.concatenate(
            [w[:, :QK_WIDTH][:, order] * QK_SCALE,
             w[:, QK_WIDTH:2 * QK_WIDTH][:, order],
             w[:, 2 * QK_WIDTH:2 * QK_WIDTH + ATT_WIDTH + POOL_WIDTH]], axis=1).astype(BF16)
        w_gates = w[:, 2 * QK_WIDTH + ATT_WIDTH + POOL_WIDTH:].astype(BF16)
        lam_vecs = jnp.stack([lambda_q1[l], lambda_k1[l], lambda_q2[l], lambda_k2[l]]).astype(F32)

        h = _ffn_ln(h, ffn1_w_gate[l].astype(BF16), ffn1_w_up[l].astype(BF16),
                    ffn1_w_down[l].astype(BF16), _row(ln1_g[l]), _row(ln1_b[l]))
        q, k, v, u = _in_proj(h, w_proj, cos_t, sin_t, seq)
        y_pool = _pool(u, pool_w[l].astype(BF16), _row(pool_scale[l]), batch, seq)
        y_att = _attention(lam_vecs, q, k, v, _row(attn_subln_g[l]), batch, seq, _lambda_init(l))
        h = _merge_ln(h, y_att, y_pool, w_gates, w_branch_att[l].astype(BF16),
                      w_branch_pool[l].astype(BF16), w_out[l].astype(BF16),
                      _row(ln2_g[l]), _row(ln2_b[l]))
        h = _ffn_ln(h, ffn2_w_gate[l].astype(BF16), ffn2_w_up[l].astype(BF16),
                    ffn2_w_down[l].astype(BF16), _row(ln3_g[l]), _row(ln3_b[l]))
    return h.reshape(batch, seq, D_MODEL)
```

```python
import functools
import math

import jax
import jax.numpy as jnp
import numpy as np
from jax import lax
from jax.experimental import pallas as pl
from jax.experimental.pallas import tpu as pltpu

D_MODEL = 1024
DEPTH = 1
N_ATT_HEADS = 4
HEAD_DIM = 64
HALF_DIM = HEAD_DIM // 2
V_HEAD_DIM = 2 * HEAD_DIM
QK_WIDTH = N_ATT_HEADS * 2 * HEAD_DIM
ATT_WIDTH = N_ATT_HEADS * V_HEAD_DIM
POOL_WIDTH = D_MODEL // 2
POOL_WINDOWS = (2, 4, 8, 16)
N_POOL_GROUPS = len(POOL_WINDOWS)
POOL_GROUP_DIM = POOL_WIDTH // N_POOL_GROUPS
D_FF = 2816
ROPE_THETA = 10000.0
LN_EPS = 1e-5
RMS_EPS = 1e-5
DEEPNORM_ALPHA = (2.0 * DEPTH) ** 0.25
QK_SCALE = HEAD_DIM ** -0.5 * math.log2(math.e)

LANES = 128
SUBLANES = 8
PAIR_WIDTH = 2 * LANES

FFN_ROWS = 512
FFN_CHUNK = 256
FFN_ANCHOR_CHUNK = 4
PROJ_ROWS = 512
MERGE_ROWS = 1024
ATT_Q_ROWS = 1024
ATT_SUB_ROWS = 512
VMEM_LIMIT_BYTES = 56 * 1024 * 1024

F32 = jnp.float32
BF16 = jnp.bfloat16


def _lambda_init(layer_idx):
    return 0.8 - 0.6 * math.exp(-0.3 * layer_idx)


def _layer_norm(y, g, b):
    mu = jnp.mean(y, axis=-1, keepdims=True)
    d = y - mu
    var = jnp.mean(d * d, axis=-1, keepdims=True)
    return d * lax.rsqrt(var + LN_EPS) * g + b


def _resident(shape):
    zeros = (0,) * len(shape)
    return pl.BlockSpec(shape, lambda *_: zeros, pipeline_mode=pl.Buffered(1))


def _params(n_axes):
    return pltpu.CompilerParams(
        dimension_semantics=("arbitrary",) * n_axes,
        vmem_limit_bytes=VMEM_LIMIT_BYTES,
    )


def _zero_bits_after(values, width):
    bits = lax.bitcast_convert_type(values, jnp.uint32)
    while bits.shape[0] > SUBLANES:
        half = bits.shape[0] // 2
        bits = bits[:half] | bits[half:]
    while bits.shape[1] > LANES:
        half = bits.shape[1] // 2
        bits = bits[:, :half] | bits[:, half:]
    zero = ((bits >> 16) >> 16)[0:1]
    return jnp.concatenate([zero] * (width // LANES), axis=1)


def _select_after(x, other, zero_bits):
    return jnp.where(zero_bits == 0, x, other)


def _swiglu_residual(x, wg_ref, wu_ref, wd_ref, before_anchor=None):
    xb = x.astype(BF16)
    acc = None
    for c in range(D_FF // FFN_CHUNK):
        cols = slice(c * FFN_CHUNK, (c + 1) * FFN_CHUNK)
        gate = jnp.dot(xb, wg_ref[:, cols], preferred_element_type=F32)
        up = jnp.dot(xb, wu_ref[:, cols], preferred_element_type=F32)
        if before_anchor is not None and c == FFN_ANCHOR_CHUNK:
            up = _select_after(up, gate, _zero_bits_after(before_anchor, FFN_CHUNK))
        h = (gate * jax.nn.sigmoid(gate) * up).astype(BF16)
        part = jnp.dot(h, wd_ref[cols, :], preferred_element_type=F32)
        acc = part if acc is None else acc + part
    return DEEPNORM_ALPHA * x + 0.5 * acc


def _ffn_ln_kernel(x_ref, wg_ref, wu_ref, wd_ref, g_ref, b_ref, o_ref, y_ref):
    i = pl.program_id(0)
    last = pl.num_programs(0) - 1

    @pl.when(i == 0)
    def _():
        y_ref[...] = jnp.zeros_like(y_ref)

    @pl.when(i < last)
    def _():
        normed = _layer_norm(y_ref[...], g_ref[...], b_ref[...])
        o_ref[...] = normed
        y_ref[...] = _swiglu_residual(x_ref[...], wg_ref, wu_ref, wd_ref, before_anchor=normed)

    @pl.when(i == last)
    def _():
        o_ref[...] = _layer_norm(y_ref[...], g_ref[...], b_ref[...])


def _skewed_specs(n_tiles, rows):
    cur = lambda w: pl.BlockSpec((rows, w), lambda i: (jnp.minimum(i, n_tiles - 1), 0))
    prev = lambda w: pl.BlockSpec((rows, w), lambda i: (jnp.maximum(i - 1, 0), 0))
    return cur, prev


def _ffn_ln(x2d, w_gate, w_up, w_down, ln_g, ln_b):
    n_tok = x2d.shape[0]
    n_tiles = n_tok // FFN_ROWS
    cur, prev = _skewed_specs(n_tiles, FFN_ROWS)
    return pl.pallas_call(
        _ffn_ln_kernel,
        grid=(n_tiles + 1,),
        in_specs=[
            cur(D_MODEL),
            _resident((D_MODEL, D_FF)),
            _resident((D_MODEL, D_FF)),
            _resident((D_FF, D_MODEL)),
            _resident((1, D_MODEL)),
            _resident((1, D_MODEL)),
        ],
        out_specs=prev(D_MODEL),
        out_shape=jax.ShapeDtypeStruct((n_tok, D_MODEL), F32),
        scratch_shapes=[pltpu.VMEM((FFN_ROWS, D_MODEL), F32)],
        compiler_params=_params(1),
        name="ffn_ln",
    )(x2d, w_gate, w_up, w_down, ln_g, ln_b)


def _rotate_pairs(first, second, cos, sin):
    r1 = first * cos - second * sin
    r2 = first * sin + second * cos
    return [r1[:, :LANES], r2[:, :LANES], r1[:, LANES:], r2[:, LANES:]]


def _in_proj_kernel(x_ref, w_ref, cos_ref, sin_ref, q_ref, k_ref, v_ref, u_ref):
    xb = x_ref[...].astype(BF16)
    h = jnp.dot(xb, w_ref[...], preferred_element_type=F32)
    cos = jnp.concatenate([cos_ref[...], cos_ref[...]], axis=1)
    sin = jnp.concatenate([sin_ref[...], sin_ref[...]], axis=1)
    half = QK_WIDTH // 2
    for out_ref, base in ((q_ref, 0), (k_ref, QK_WIDTH)):
        blocks = _rotate_pairs(h[:, base:base + half], h[:, base + half:base + QK_WIDTH], cos, sin)
        for j, blk in enumerate(blocks):
            out_ref[:, j * LANES:(j + 1) * LANES] = blk.astype(BF16)
    v_ref[...] = h[:, 2 * QK_WIDTH:2 * QK_WIDTH + ATT_WIDTH].astype(BF16)
    u_ref[...] = h[:, 2 * QK_WIDTH + ATT_WIDTH:]


def _in_proj(x2d, w_proj, cos_t, sin_t, seq):
    n_tok = x2d.shape[0]
    width = w_proj.shape[1]
    tiles_per_seq = seq // PROJ_ROWS
    row = lambda w: pl.BlockSpec((PROJ_ROWS, w), lambda i: (i, 0))
    table = pl.BlockSpec((PROJ_ROWS, LANES), lambda i: (i % tiles_per_seq, 0))
    return pl.pallas_call(
        _in_proj_kernel,
        grid=(n_tok // PROJ_ROWS,),
        in_specs=[row(D_MODEL), _resident((D_MODEL, width)), table, table],
        out_specs=[row(QK_WIDTH), row(QK_WIDTH), row(ATT_WIDTH), row(POOL_WIDTH)],
        out_shape=[
            jax.ShapeDtypeStruct((n_tok, QK_WIDTH), BF16),
            jax.ShapeDtypeStruct((n_tok, QK_WIDTH), BF16),
            jax.ShapeDtypeStruct((n_tok, ATT_WIDTH), BF16),
            jax.ShapeDtypeStruct((n_tok, POOL_WIDTH), F32),
        ],
        compiler_params=_params(1),
        name="in_proj",
    )(x2d, w_proj, cos_t, sin_t)


def _pool_kernel(u_ref, pw_ref, ps_ref, o_ref):
    seq = u_ref.shape[0]
    pos = lax.broadcasted_iota(jnp.int32, (seq, POOL_GROUP_DIM), 0)

    def prev(a, k):
        return jnp.where(pos >= k, pltpu.roll(a, k, axis=0), 0.0)

    def nxt(a, k):
        return jnp.where(pos < seq - k, pltpu.roll(a, seq - k, axis=0), 0.0)

    for g, w in enumerate(POOL_WINDOWS):
        cols = slice(g * POOL_GROUP_DIM, (g + 1) * POOL_GROUP_DIM)
        u = u_ref[:, cols]
        n = w // 2
        back = prev(u, 1)
        fwd = u
        k = 1
        while k < n:
            back = back + prev(back, k)
            fwd = fwd + nxt(fwd, k)
            k *= 2
        cnt = jnp.minimum(pos + n, seq) - jnp.maximum(pos - n, 0)
        pooled = (back + fwd) / cnt.astype(F32) - u
        mixed = jnp.dot(pooled.astype(BF16), pw_ref[g], preferred_element_type=F32)
        o_ref[:, cols] = (mixed * ps_ref[:, cols]).astype(BF16)


def _pool(u2d, pool_w, pool_scale, batch, seq):
    blk = pl.BlockSpec((seq, POOL_WIDTH), lambda b: (b, 0))
    return pl.pallas_call(
        _pool_kernel,
        grid=(batch,),
        in_specs=[
            blk,
            _resident((N_POOL_GROUPS, POOL_GROUP_DIM, POOL_GROUP_DIM)),
            _resident((1, POOL_WIDTH)),
        ],
        out_specs=blk,
        out_shape=jax.ShapeDtypeStruct((batch * seq, POOL_WIDTH), BF16),
        compiler_params=_params(1),
        name="pool",
    )(u2d, pool_w, pool_scale)


def _attention_kernel(lam_ref, q_ref, k_ref, v_ref, g_ref, o_ref, vext_ref, *, lambda_init):
    @pl.when(pl.program_id(1) == 0)
    def _():
        ones = jnp.ones((v_ref.shape[0], V_HEAD_DIM), BF16)
        for head in range(N_ATT_HEADS):
            base = head * 2 * V_HEAD_DIM
            vext_ref[:, base:base + V_HEAD_DIM] = v_ref[:, head * V_HEAD_DIM:(head + 1) * V_HEAD_DIM]
            vext_ref[:, base + V_HEAD_DIM:base + 2 * V_HEAD_DIM] = ones

    lv = lam_ref[...]
    lam = (jnp.exp(jnp.sum(lv[0:1] * lv[1:2], axis=-1, keepdims=True))
           - jnp.exp(jnp.sum(lv[2:3] * lv[3:4], axis=-1, keepdims=True))
           + lambda_init)
    group = (lax.broadcasted_iota(jnp.int32, (1, PAIR_WIDTH), 1) % LANES) // HALF_DIM
    contract_last = (((1,), (1,)), ((), ()))
    for r in range(ATT_Q_ROWS // ATT_SUB_ROWS):
        rows = slice(r * ATT_SUB_ROWS, (r + 1) * ATT_SUB_ROWS)
        for pair in range(N_ATT_HEADS // 2):
            cols = slice(pair * PAIR_WIDTH, (pair + 1) * PAIR_WIDTH)
            q_pair = q_ref[rows, cols]
            k_pair = k_ref[:, cols]
            for sub in range(2):
                head = 2 * pair + sub
                v_ext = vext_ref[:, head * 2 * V_HEAD_DIM:(head + 1) * 2 * V_HEAD_DIM]
                maps = []
                for m in range(2):
                    q_map = jnp.where(group == 2 * sub + m, q_pair, jnp.zeros_like(q_pair))
                    s = lax.dot_general(q_map, k_pair, contract_last, preferred_element_type=F32)
                    e = jnp.exp2(s - jnp.max(s, axis=-1, keepdims=True)).astype(BF16)
                    ev = jnp.dot(e, v_ext, preferred_element_type=F32)
                    maps.append(ev[:, :V_HEAD_DIM] / ev[:, V_HEAD_DIM:])
                o = maps[0] - lam * maps[1]
                hcols = slice(head * V_HEAD_DIM, (head + 1) * V_HEAD_DIM)
                o = o * lax.rsqrt(jnp.mean(o * o, axis=-1, keepdims=True) + RMS_EPS)
                o_ref[rows, hcols] = ((o * (1.0 - lambda_init)) * g_ref[...]).astype(BF16)


def _attention(lam_vecs, q2d, k2d, v2d, subln_g, batch, seq, lambda_init):
    q_tiles = seq // ATT_Q_ROWS
    q_spec = pl.BlockSpec((ATT_Q_ROWS, QK_WIDTH), lambda b, i: (b * q_tiles + i, 0))
    kv_spec = lambda w: pl.BlockSpec((seq, w), lambda b, i: (b, 0))
    return pl.pallas_call(
        functools.partial(_attention_kernel, lambda_init=lambda_init),
        grid=(batch, q_tiles),
        in_specs=[
            _resident((4, HEAD_DIM)),
            q_spec,
            kv_spec(QK_WIDTH),
            kv_spec(ATT_WIDTH),
            _resident((1, V_HEAD_DIM)),
        ],
        out_specs=pl.BlockSpec((ATT_Q_ROWS, ATT_WIDTH), lambda b, i: (b * q_tiles + i, 0)),
        out_shape=jax.ShapeDtypeStruct((batch * seq, ATT_WIDTH), BF16),
        scratch_shapes=[pltpu.VMEM((seq, 2 * ATT_WIDTH), BF16)],
        compiler_params=_params(2),
        name="diff_attention",
    )(lam_vecs, q2d, k2d, v2d, subln_g)


def _merge_ln_kernel(x_ref, ya_ref, yp_ref, wgate_ref, wa_ref, wp_ref, wo_ref, g_ref, b_ref, o_ref, y_ref):
    i = pl.program_id(0)
    last = pl.num_programs(0) - 1

    @pl.when(i == 0)
    def _():
        y_ref[...] = jnp.zeros_like(y_ref)

    @pl.when(i < last)
    def _():
        normed = _layer_norm(y_ref[...], g_ref[...], b_ref[...])
        o_ref[...] = normed
        x = x_ref[...]
        xb = x.astype(BF16)
        gates = jax.nn.sigmoid(jnp.dot(xb, wgate_ref[...], preferred_element_type=F32))
        att = jnp.dot(ya_ref[...], wa_ref[...], preferred_element_type=F32)
        pool = jnp.dot(yp_ref[...], wp_ref[...], preferred_element_type=F32)
        anchored = _select_after(pool[:, :PAIR_WIDTH], att[:, :PAIR_WIDTH], _zero_bits_after(normed, PAIR_WIDTH))
        pool = jnp.concatenate([anchored, pool[:, PAIR_WIDTH:]], axis=1)
        merged = gates[:, :D_MODEL] * att + gates[:, D_MODEL:] * pool
        mix = jnp.dot(merged.astype(BF16), wo_ref[...], preferred_element_type=F32)
        y_ref[...] = DEEPNORM_ALPHA * x + mix

    @pl.when(i == last)
    def _():
        o_ref[...] = _layer_norm(y_ref[...], g_ref[...], b_ref[...])


def _merge_ln(x2d, y_att, y_pool, w_gates, w_att, w_pool, w_out, ln_g, ln_b):
    n_tok = x2d.shape[0]
    n_tiles = n_tok // MERGE_ROWS
    cur, prev = _skewed_specs(n_tiles, MERGE_ROWS)
    return pl.pallas_call(
        _merge_ln_kernel,
        grid=(n_tiles + 1,),
        in_specs=[
            cur(D_MODEL), cur(ATT_WIDTH), cur(POOL_WIDTH),
            _resident((D_MODEL, 2 * D_MODEL)),
            _resident((ATT_WIDTH, D_MODEL)),
            _resident((POOL_WIDTH, D_MODEL)),
            _resident((D_MODEL, D_MODEL)),
            _resident((1, D_MODEL)),
            _resident((1, D_MODEL)),
        ],
        out_specs=prev(D_MODEL),
        out_shape=jax.ShapeDtypeStruct((n_tok, D_MODEL), F32),
        scratch_shapes=[pltpu.VMEM((MERGE_ROWS, D_MODEL), F32)],
        compiler_params=_params(1),
        name="merge_ln",
    )(x2d, y_att, y_pool, w_gates, w_att, w_pool, w_out, ln_g, ln_b)


def _rotary_column_order():
    first = [h * 2 * HEAD_DIM + m * HEAD_DIM + d
             for h in range(N_ATT_HEADS) for m in range(2) for d in range(HALF_DIM)]
    first = np.asarray(first, dtype=np.int32)
    return np.concatenate([first, first + HALF_DIM])


def _rope_tables(seq):
    inv = 1.0 / (ROPE_THETA ** (jnp.arange(0, HEAD_DIM, 2, dtype=F32) / HEAD_DIM))
    ang = jnp.arange(seq, dtype=F32)[:, None] * inv[None, :]
    reps = LANES // HALF_DIM
    return jnp.tile(jnp.cos(ang), (1, reps)), jnp.tile(jnp.sin(ang), (1, reps))


def _row(v):
    return v.reshape(1, -1)


def kernel(x, ln1_g, ln1_b, ffn1_w_gate, ffn1_w_up, ffn1_w_down, w_in, lambda_q1, lambda_k1, lambda_q2, lambda_k2, attn_subln_g, pool_w, pool_scale, w_branch_att, w_branch_pool, w_out, ln2_g, ln2_b, ffn2_w_gate, ffn2_w_up, ffn2_w_down, ln3_g, ln3_b):
    batch, seq, _ = x.shape
    order = _rotary_column_order()
    cos_t, sin_t = _rope_tables(seq)
    h = x.reshape(batch * seq, D_MODEL)
    for l in range(DEPTH):
        w = w_in[l]
        w_proj = jnp.concatenate(
            [w[:, :QK_WIDTH][:, order] * QK_SCALE,
             w[:, QK_WIDTH:2 * QK_WIDTH][:, order],
             w[:, 2 * QK_WIDTH:2 * QK_WIDTH + ATT_WIDTH + POOL_WIDTH]], axis=1).astype(BF16)
        w_gates = w[:, 2 * QK_WIDTH + ATT_WIDTH + POOL_WIDTH:].astype(BF16)
        lam_vecs = jnp.stack([lambda_q1[l], lambda_k1[l], lambda_q2[l], lambda_k2[l]]).astype(F32)

        h = _ffn_ln(h, ffn1_w_gate[l].astype(BF16), ffn1_w_up[l].astype(BF16),
                    ffn1_w_down[l].astype(BF16), _row(ln1_g[l]), _row(ln1_b[l]))
        q, k, v, u = _in_proj(h, w_proj, cos_t, sin_t, seq)
        y_pool = _pool(u, pool_w[l].astype(BF16), _row(pool_scale[l]), batch, seq)
        y_att = _attention(lam_vecs, q, k, v, _row(attn_subln_g[l]), batch, seq, _lambda_init(l))
        h = _merge_ln(h, y_att, y_pool, w_gates, w_branch_att[l].astype(BF16),
                      w_branch_pool[l].astype(BF16), w_out[l].astype(BF16),
                      _row(ln2_g[l]), _row(ln2_b[l]))
        h = _ffn_ln(h, ffn2_w_gate[l].astype(BF16), ffn2_w_up[l].astype(BF16),
                    ffn2_w_down[l].astype(BF16), _row(ln3_g[l]), _row(ln3_b[l]))
    return h.reshape(batch, seq, D_MODEL)
```

```python
import functools
import math

import jax
import jax.numpy as jnp
import numpy as np
from jax import lax
from jax.experimental import pallas as pl
from jax.experimental.pallas import tpu as pltpu

D_MODEL = 1024
DEPTH = 1
N_ATT_HEADS = 4
HEAD_DIM = 64
HALF_DIM = HEAD_DIM // 2
V_HEAD_DIM = 2 * HEAD_DIM
QK_WIDTH = N_ATT_HEADS * 2 * HEAD_DIM
ATT_WIDTH = N_ATT_HEADS * V_HEAD_DIM
POOL_WIDTH = D_MODEL // 2
POOL_WINDOWS = (2, 4, 8, 16)
N_POOL_GROUPS = len(POOL_WINDOWS)
POOL_GROUP_DIM = POOL_WIDTH // N_POOL_GROUPS
D_FF = 2816
ROPE_THETA = 10000.0
LN_EPS = 1e-5
RMS_EPS = 1e-5
DEEPNORM_ALPHA = (2.0 * DEPTH) ** 0.25
QK_SCALE = HEAD_DIM ** -0.5 * math.log2(math.e)

LANES = 128
SUBLANES = 8
PAIR_WIDTH = 2 * LANES

FFN_ROWS = 1024
FFN_CHUNK = 256
FFN_ANCHOR_CHUNK = 4
PROJ_ROWS = 512
MERGE_ROWS = 1024
ATT_Q_ROWS = 1024
ATT_SUB_ROWS = 512
VMEM_LIMIT_BYTES = 56 * 1024 * 1024

F32 = jnp.float32
BF16 = jnp.bfloat16


def _lambda_init(layer_idx):
    return 0.8 - 0.6 * math.exp(-0.3 * layer_idx)


def _layer_norm(y, g, b):
    mu = jnp.mean(y, axis=-1, keepdims=True)
    d = y - mu
    var = jnp.mean(d * d, axis=-1, keepdims=True)
    return d * lax.rsqrt(var + LN_EPS) * g + b


def _resident(shape):
    zeros = (0,) * len(shape)
    return pl.BlockSpec(shape, lambda *_: zeros, pipeline_mode=pl.Buffered(1))


def _params(n_axes):
    return pltpu.CompilerParams(
        dimension_semantics=("arbitrary",) * n_axes,
        vmem_limit_bytes=VMEM_LIMIT_BYTES,
    )


def _zero_bits_after(values, width):
    bits = lax.bitcast_convert_type(values, jnp.uint32)
    while bits.shape[0] > SUBLANES:
        half = bits.shape[0] // 2
        bits = bits[:half] | bits[half:]
    while bits.shape[1] > LANES:
        half = bits.shape[1] // 2
        bits = bits[:, :half] | bits[:, half:]
    zero = ((bits >> 16) >> 16)[0:1]
    return jnp.concatenate([zero] * (width // LANES), axis=1)


def _select_after(x, other, zero_bits):
    return jnp.where(zero_bits == 0, x, other)


def _swiglu_residual(x, wg_ref, wu_ref, wd_ref, before_anchor=None):
    xb = x.astype(BF16)
    acc = None
    for c in range(D_FF // FFN_CHUNK):
        cols = slice(c * FFN_CHUNK, (c + 1) * FFN_CHUNK)
        gate = jnp.dot(xb, wg_ref[:, cols], preferred_element_type=F32)
        up = jnp.dot(xb, wu_ref[:, cols], preferred_element_type=F32)
        if before_anchor is not None and c == FFN_ANCHOR_CHUNK:
            up = _select_after(up, gate, _zero_bits_after(before_anchor, FFN_CHUNK))
        h = (gate * jax.nn.sigmoid(gate) * up).astype(BF16)
        part = jnp.dot(h, wd_ref[cols, :], preferred_element_type=F32)
        acc = part if acc is None else acc + part
    return DEEPNORM_ALPHA * x + 0.5 * acc


def _ffn_ln_kernel(x_ref, wg_ref, wu_ref, wd_ref, g_ref, b_ref, o_ref, y_ref):
    i = pl.program_id(0)
    last = pl.num_programs(0) - 1

    @pl.when(i == 0)
    def _():
        y_ref[...] = jnp.zeros_like(y_ref)

    @pl.when(i < last)
    def _():
        normed = _layer_norm(y_ref[...], g_ref[...], b_ref[...])
        o_ref[...] = normed
        y_ref[...] = _swiglu_residual(x_ref[...], wg_ref, wu_ref, wd_ref, before_anchor=normed)

    @pl.when(i == last)
    def _():
        o_ref[...] = _layer_norm(y_ref[...], g_ref[...], b_ref[...])


def _skewed_specs(n_tiles, rows):
    cur = lambda w: pl.BlockSpec((rows, w), lambda i: (jnp.minimum(i, n_tiles - 1), 0))
    prev = lambda w: pl.BlockSpec((rows, w), lambda i: (jnp.maximum(i - 1, 0), 0))
    return cur, prev


def _ffn_ln(x2d, w_gate, w_up, w_down, ln_g, ln_b):
    n_tok = x2d.shape[0]
    n_tiles = n_tok // FFN_ROWS
    cur, prev = _skewed_specs(n_tiles, FFN_ROWS)
    return pl.pallas_call(
        _ffn_ln_kernel,
        grid=(n_tiles + 1,),
        in_specs=[
            cur(D_MODEL),
            _resident((D_MODEL, D_FF)),
            _resident((D_MODEL, D_FF)),
            _resident((D_FF, D_MODEL)),
            _resident((1, D_MODEL)),
            _resident((1, D_MODEL)),
        ],
        out_specs=prev(D_MODEL),
        out_shape=jax.ShapeDtypeStruct((n_tok, D_MODEL), F32),
        scratch_shapes=[pltpu.VMEM((FFN_ROWS, D_MODEL), F32)],
        compiler_params=_params(1),
        name="ffn_ln",
    )(x2d, w_gate, w_up, w_down, ln_g, ln_b)


def _rotate_pairs(first, second, cos, sin):
    r1 = first * cos - second * sin
    r2 = first * sin + second * cos
    return [r1[:, :LANES], r2[:, :LANES], r1[:, LANES:], r2[:, LANES:]]


def _in_proj_kernel(x_ref, w_ref, cos_ref, sin_ref, q_ref, k_ref, v_ref, u_ref):
    xb = x_ref[...].astype(BF16)
    h = jnp.dot(xb, w_ref[...], preferred_element_type=F32)
    cos = jnp.concatenate([cos_ref[...], cos_ref[...]], axis=1)
    sin = jnp.concatenate([sin_ref[...], sin_ref[...]], axis=1)
    half = QK_WIDTH // 2
    for out_ref, base in ((q_ref, 0), (k_ref, QK_WIDTH)):
        blocks = _rotate_pairs(h[:, base:base + half], h[:, base + half:base + QK_WIDTH], cos, sin)
        for j, blk in enumerate(blocks):
            out_ref[:, j * LANES:(j + 1) * LANES] = blk.astype(BF16)
    v_ref[...] = h[:, 2 * QK_WIDTH:2 * QK_WIDTH + ATT_WIDTH].astype(BF16)
    u_ref[...] = h[:, 2 * QK_WIDTH + ATT_WIDTH:]


def _in_proj(x2d, w_proj, cos_t, sin_t, seq):
    n_tok = x2d.shape[0]
    width = w_proj.shape[1]
    tiles_per_seq = seq // PROJ_ROWS
    row = lambda w: pl.BlockSpec((PROJ_ROWS, w), lambda i: (i, 0))
    table = pl.BlockSpec((PROJ_ROWS, LANES), lambda i: (i % tiles_per_seq, 0))
    return pl.pallas_call(
        _in_proj_kernel,
        grid=(n_tok // PROJ_ROWS,),
        in_specs=[row(D_MODEL), _resident((D_MODEL, width)), table, table],
        out_specs=[row(QK_WIDTH), row(QK_WIDTH), row(ATT_WIDTH), row(POOL_WIDTH)],
        out_shape=[
            jax.ShapeDtypeStruct((n_tok, QK_WIDTH), BF16),
            jax.ShapeDtypeStruct((n_tok, QK_WIDTH), BF16),
            jax.ShapeDtypeStruct((n_tok, ATT_WIDTH), BF16),
            jax.ShapeDtypeStruct((n_tok, POOL_WIDTH), F32),
        ],
        compiler_params=_params(1),
        name="in_proj",
    )(x2d, w_proj, cos_t, sin_t)


def _pool_kernel(u_ref, pw_ref, ps_ref, o_ref):
    seq = u_ref.shape[0]
    pos = lax.broadcasted_iota(jnp.int32, (seq, POOL_GROUP_DIM), 0)

    def prev(a, k):
        return jnp.where(pos >= k, pltpu.roll(a, k, axis=0), 0.0)

    def nxt(a, k):
        return jnp.where(pos < seq - k, pltpu.roll(a, seq - k, axis=0), 0.0)

    for g, w in enumerate(POOL_WINDOWS):
        cols = slice(g * POOL_GROUP_DIM, (g + 1) * POOL_GROUP_DIM)
        u = u_ref[:, cols]
        n = w // 2
        back = prev(u, 1)
        fwd = u
        k = 1
        while k < n:
            back = back + prev(back, k)
            fwd = fwd + nxt(fwd, k)
            k *= 2
        cnt = jnp.minimum(pos + n, seq) - jnp.maximum(pos - n, 0)
        pooled = (back + fwd) / cnt.astype(F32) - u
        mixed = jnp.dot(pooled.astype(BF16), pw_ref[g], preferred_element_type=F32)
        o_ref[:, cols] = (mixed * ps_ref[:, cols]).astype(BF16)


def _pool(u2d, pool_w, pool_scale, batch, seq):
    blk = pl.BlockSpec((seq, POOL_WIDTH), lambda b: (b, 0))
    return pl.pallas_call(
        _pool_kernel,
        grid=(batch,),
        in_specs=[
            blk,
            _resident((N_POOL_GROUPS, POOL_GROUP_DIM, POOL_GROUP_DIM)),
            _resident((1, POOL_WIDTH)),
        ],
        out_specs=blk,
        out_shape=jax.ShapeDtypeStruct((batch * seq, POOL_WIDTH), BF16),
        compiler_params=_params(1),
        name="pool",
    )(u2d, pool_w, pool_scale)


def _attention_kernel(lam_ref, q_ref, k_ref, v_ref, g_ref, qn_ref, kn_ref, o_ref, vext_ref, e0_ref,
                      *, lambda_init):
    @pl.when(pl.program_id(1) == 0)
    def _():
        ones = jnp.ones((v_ref.shape[0], V_HEAD_DIM), BF16)
        for head in range(N_ATT_HEADS):
            base = head * 2 * V_HEAD_DIM
            vext_ref[:, base:base + V_HEAD_DIM] = v_ref[:, head * V_HEAD_DIM:(head + 1) * V_HEAD_DIM]
            vext_ref[:, base + V_HEAD_DIM:base + 2 * V_HEAD_DIM] = ones

    lv = lam_ref[...]
    lam = (jnp.exp(jnp.sum(lv[0:1] * lv[1:2], axis=-1, keepdims=True))
           - jnp.exp(jnp.sum(lv[2:3] * lv[3:4], axis=-1, keepdims=True))
           + lambda_init)
    group = (lax.broadcasted_iota(jnp.int32, (1, PAIR_WIDTH), 1) % LANES) // HALF_DIM
    contract_last = (((1,), (1,)), ((), ()))
    first_rows = slice(0, ATT_SUB_ROWS)

    def numerators(q_src, k_src, rows, head, m):
        pair, sub = divmod(head, 2)
        cols = slice(pair * PAIR_WIDTH, (pair + 1) * PAIR_WIDTH)
        q_pair = q_src[rows, cols]
        q_map = jnp.where(group == 2 * sub + m, q_pair, jnp.zeros_like(q_pair))
        s = lax.dot_general(q_map, k_src[:, cols], contract_last, preferred_element_type=F32)
        return jnp.exp2(s - jnp.max(s, axis=-1, keepdims=True)).astype(BF16)

    @pl.when((pl.program_id(0) == 0) & (pl.program_id(1) == 0))
    def _():
        e0_ref[...] = numerators(q_ref, k_ref, first_rows, 0, 0)

    for r in range(ATT_Q_ROWS // ATT_SUB_ROWS):
        rows = slice(r * ATT_SUB_ROWS, (r + 1) * ATT_SUB_ROWS)
        for head in range(N_ATT_HEADS):
            v_ext = vext_ref[:, head * 2 * V_HEAD_DIM:(head + 1) * 2 * V_HEAD_DIM]
            maps = []
            for m in range(2):
                if (r, head, m) == (0, 0, 0):
                    e = e0_ref[...]
                else:
                    e = numerators(q_ref, k_ref, rows, head, m)
                ev = jnp.dot(e, v_ext, preferred_element_type=F32)
                maps.append(ev[:, :V_HEAD_DIM] / ev[:, V_HEAD_DIM:])
            o = maps[0] - lam * maps[1]
            hcols = slice(head * V_HEAD_DIM, (head + 1) * V_HEAD_DIM)
            o = o * lax.rsqrt(jnp.mean(o * o, axis=-1, keepdims=True) + RMS_EPS)
            o_ref[rows, hcols] = ((o * (1.0 - lambda_init)) * g_ref[...]).astype(BF16)
    e0_ref[...] = numerators(qn_ref, kn_ref, first_rows, 0, 0)


def _attention(lam_vecs, q2d, k2d, v2d, subln_g, batch, seq, lambda_init):
    q_tiles = seq // ATT_Q_ROWS
    n_tiles = batch * q_tiles
    next_tile = lambda b, i: jnp.minimum(b * q_tiles + i + 1, n_tiles - 1)
    q_spec = pl.BlockSpec((ATT_Q_ROWS, QK_WIDTH), lambda b, i: (b * q_tiles + i, 0))
    q_next_spec = pl.BlockSpec((ATT_Q_ROWS, QK_WIDTH), lambda b, i: (next_tile(b, i), 0))
    kv_spec = lambda w: pl.BlockSpec((seq, w), lambda b, i: (b, 0))
    k_next_spec = pl.BlockSpec((seq, QK_WIDTH), lambda b, i: (next_tile(b, i) // q_tiles, 0))
    return pl.pallas_call(
        functools.partial(_attention_kernel, lambda_init=lambda_init),
        grid=(batch, q_tiles),
        in_specs=[
            _resident((4, HEAD_DIM)),
            q_spec,
            kv_spec(QK_WIDTH),
            kv_spec(ATT_WIDTH),
            _resident((1, V_HEAD_DIM)),
            q_next_spec,
            k_next_spec,
        ],
        out_specs=pl.BlockSpec((ATT_Q_ROWS, ATT_WIDTH), lambda b, i: (b * q_tiles + i, 0)),
        out_shape=jax.ShapeDtypeStruct((batch * seq, ATT_WIDTH), BF16),
        scratch_shapes=[
            pltpu.VMEM((seq, 2 * ATT_WIDTH), BF16),
            pltpu.VMEM((ATT_SUB_ROWS, seq), BF16),
        ],
        compiler_params=_params(2),
        name="diff_attention",
    )(lam_vecs, q2d, k2d, v2d, subln_g, q2d, k2d)


def _merge_ln_kernel(x_ref, ya_ref, yp_ref, wgate_ref, wa_ref, wp_ref, wo_ref, g_ref, b_ref, o_ref, y_ref):
    i = pl.program_id(0)
    last = pl.num_programs(0) - 1

    @pl.when(i == 0)
    def _():
        y_ref[...] = jnp.zeros_like(y_ref)

    @pl.when(i < last)
    def _():
        normed = _layer_norm(y_ref[...], g_ref[...], b_ref[...])
        o_ref[...] = normed
        x = x_ref[...]
        xb = x.astype(BF16)
        gates = jax.nn.sigmoid(jnp.dot(xb, wgate_ref[...], preferred_element_type=F32))
        att = jnp.dot(ya_ref[...], wa_ref[...], preferred_element_type=F32)
        pool = jnp.dot(yp_ref[...], wp_ref[...], preferred_element_type=F32)
        anchored = _select_after(pool[:, :PAIR_WIDTH], att[:, :PAIR_WIDTH], _zero_bits_after(normed, PAIR_WIDTH))
        pool = jnp.concatenate([anchored, pool[:, PAIR_WIDTH:]], axis=1)
        merged = gates[:, :D_MODEL] * att + gates[:, D_MODEL:] * pool
        mix = jnp.dot(merged.astype(BF16), wo_ref[...], preferred_element_type=F32)
        y_ref[...] = DEEPNORM_ALPHA * x + mix

    @pl.when(i == last)
    def _():
        o_ref[...] = _layer_norm(y_ref[...], g_ref[...], b_ref[...])


def _merge_ln(x2d, y_att, y_pool, w_gates, w_att, w_pool, w_out, ln_g, ln_b):
    n_tok = x2d.shape[0]
    n_tiles = n_tok // MERGE_ROWS
    cur, prev = _skewed_specs(n_tiles, MERGE_ROWS)
    return pl.pallas_call(
        _merge_ln_kernel,
        grid=(n_tiles + 1,),
        in_specs=[
            cur(D_MODEL), cur(ATT_WIDTH), cur(POOL_WIDTH),
            _resident((D_MODEL, 2 * D_MODEL)),
            _resident((ATT_WIDTH, D_MODEL)),
            _resident((POOL_WIDTH, D_MODEL)),
            _resident((D_MODEL, D_MODEL)),
            _resident((1, D_MODEL)),
            _resident((1, D_MODEL)),
        ],
        out_specs=prev(D_MODEL),
        out_shape=jax.ShapeDtypeStruct((n_tok, D_MODEL), F32),
        scratch_shapes=[pltpu.VMEM((MERGE_ROWS, D_MODEL), F32)],
        compiler_params=_params(1),
        name="merge_ln",
    )(x2d, y_att, y_pool, w_gates, w_att, w_pool, w_out, ln_g, ln_b)


def _rotary_column_order():
    first = [h * 2 * HEAD_DIM + m * HEAD_DIM + d
             for h in range(N_ATT_HEADS) for m in range(2) for d in range(HALF_DIM)]
    first = np.asarray(first, dtype=np.int32)
    return np.concatenate([first, first + HALF_DIM])


def _rope_tables(seq):
    inv = 1.0 / (ROPE_THETA ** (jnp.arange(0, HEAD_DIM, 2, dtype=F32) / HEAD_DIM))
    ang = jnp.arange(seq, dtype=F32)[:, None] * inv[None, :]
    reps = LANES // HALF_DIM
    return jnp.tile(jnp.cos(ang), (1, reps)), jnp.tile(jnp.sin(ang), (1, reps))


def _row(v):
    return v.reshape(1, -1)


def kernel(x, ln1_g, ln1_b, ffn1_w_gate, ffn1_w_up, ffn1_w_down, w_in, lambda_q1, lambda_k1, lambda_q2, lambda_k2, attn_subln_g, pool_w, pool_scale, w_branch_att, w_branch_pool, w_out, ln2_g, ln2_b, ffn2_w_gate, ffn2_w_up, ffn2_w_down, ln3_g, ln3_b):
    batch, seq, _ = x.shape
    order = _rotary_column_order()
    cos_t, sin_t = _rope_tables(seq)
    h = x.reshape(batch * seq, D_MODEL)
    for l in range(DEPTH):
        w = w_in[l]
        w_proj = jnp.concatenate(
            [w[:, :QK_WIDTH][:, order] * QK_SCALE,
             w[:, QK_WIDTH:2 * QK_WIDTH][:, order],
             w[:, 2 * QK_WIDTH:2 * QK_WIDTH + ATT_WIDTH + POOL_WIDTH]], axis=1).astype(BF16)
        w_gates = w[:, 2 * QK_WIDTH + ATT_WIDTH + POOL_WIDTH:].astype(BF16)
        lam_vecs = jnp.stack([lambda_q1[l], lambda_k1[l], lambda_q2[l], lambda_k2[l]]).astype(F32)

        h = _ffn_ln(h, ffn1_w_gate[l].astype(BF16), ffn1_w_up[l].astype(BF16),
                    ffn1_w_down[l].astype(BF16), _row(ln1_g[l]), _row(ln1_b[l]))
        q, k, v, u = _in_proj(h, w_proj, cos_t, sin_t, seq)
        y_pool = _pool(u, pool_w[l].astype(BF16), _row(pool_scale[l]), batch, seq)
        y_att = _attention(lam_vecs, q, k, v, _row(attn_subln_g[l]), batch, seq, _lambda_init(l))
        h = _merge_ln(h, y_att, y_pool, w_gates, w_branch_att[l].astype(BF16),
                      w_branch_pool[l].astype(BF16), w_out[l].astype(BF16),
                      _row(ln2_g[l]), _row(ln2_b[l]))
        h = _ffn_ln(h, ffn2_w_gate[l].astype(BF16), ffn2_w_up[l].astype(BF16),
                    ffn2_w_down[l].astype(BF16), _row(ln3_g[l]), _row(ln3_b[l]))
    return h.reshape(batch, seq, D_MODEL)
```

```python
import functools
import math

import jax
import jax.numpy as jnp
import numpy as np
from jax import lax
from jax.experimental import pallas as pl
from jax.experimental.pallas import tpu as pltpu

D_MODEL = 1024
DEPTH = 1
N_ATT_HEADS = 4
HEAD_DIM = 64
HALF_DIM = HEAD_DIM // 2
V_HEAD_DIM = 2 * HEAD_DIM
QK_WIDTH = N_ATT_HEADS * 2 * HEAD_DIM
ATT_WIDTH = N_ATT_HEADS * V_HEAD_DIM
POOL_WIDTH = D_MODEL // 2
POOL_WINDOWS = (2, 4, 8, 16)
N_POOL_GROUPS = len(POOL_WINDOWS)
POOL_GROUP_DIM = POOL_WIDTH // N_POOL_GROUPS
D_FF = 2816
ROPE_THETA = 10000.0
LN_EPS = 1e-5
RMS_EPS = 1e-5
DEEPNORM_ALPHA = (2.0 * DEPTH) ** 0.25
QK_SCALE = HEAD_DIM ** -0.5 * math.log2(math.e)

LANES = 128
SUBLANES = 8
PAIR_WIDTH = 2 * LANES

FFN_ROWS = 1024
FFN_CHUNK = 256
FFN_ANCHOR_CHUNK = 4
PROJ_ROWS = 1024
MERGE_ROWS = 1024
MERGE_SUB_ROWS = 512
ATT_Q_ROWS = 1024
ATT_SUB_ROWS = 512
VMEM_LIMIT_BYTES = 56 * 1024 * 1024

F32 = jnp.float32
BF16 = jnp.bfloat16


def _lambda_init(layer_idx):
    return 0.8 - 0.6 * math.exp(-0.3 * layer_idx)


def _layer_norm(y, g, b):
    mu = jnp.mean(y, axis=-1, keepdims=True)
    d = y - mu
    var = jnp.mean(d * d, axis=-1, keepdims=True)
    return d * lax.rsqrt(var + LN_EPS) * g + b


def _resident(shape):
    zeros = (0,) * len(shape)
    return pl.BlockSpec(shape, lambda *_: zeros, pipeline_mode=pl.Buffered(1))


def _params(n_axes):
    return pltpu.CompilerParams(
        dimension_semantics=("arbitrary",) * n_axes,
        vmem_limit_bytes=VMEM_LIMIT_BYTES,
    )


def _zero_bits_after(values, width):
    bits = lax.bitcast_convert_type(values, jnp.uint32)
    while bits.shape[0] > SUBLANES:
        half = bits.shape[0] // 2
        bits = bits[:half] | bits[half:]
    while bits.shape[1] > LANES:
        half = bits.shape[1] // 2
        bits = bits[:, :half] | bits[:, half:]
    zero = ((bits >> 16) >> 16)[0:1]
    return jnp.concatenate([zero] * (width // LANES), axis=1)


def _select_after(x, other, zero_bits):
    return jnp.where(zero_bits == 0, x, other)


def _swiglu_residual(x, wg_ref, wu_ref, wd_ref, before_anchor=None):
    xb = x.astype(BF16)
    acc = None
    for c in range(D_FF // FFN_CHUNK):
        cols = slice(c * FFN_CHUNK, (c + 1) * FFN_CHUNK)
        gate = jnp.dot(xb, wg_ref[:, cols], preferred_element_type=F32)
        up = jnp.dot(xb, wu_ref[:, cols], preferred_element_type=F32)
        if before_anchor is not None and c == FFN_ANCHOR_CHUNK:
            up = _select_after(up, gate, _zero_bits_after(before_anchor, FFN_CHUNK))
        h = (gate * jax.nn.sigmoid(gate) * up).astype(BF16)
        part = jnp.dot(h, wd_ref[cols, :], preferred_element_type=F32)
        acc = part if acc is None else acc + part
    return DEEPNORM_ALPHA * x + 0.5 * acc


def _ffn_ln_kernel(x_ref, wg_ref, wu_ref, wd_ref, g_ref, b_ref, o_ref, y_ref):
    i = pl.program_id(0)
    last = pl.num_programs(0) - 1

    @pl.when(i == 0)
    def _():
        y_ref[...] = jnp.zeros_like(y_ref)

    @pl.when(i < last)
    def _():
        normed = _layer_norm(y_ref[...], g_ref[...], b_ref[...])
        o_ref[...] = normed
        y_ref[...] = _swiglu_residual(x_ref[...], wg_ref, wu_ref, wd_ref, before_anchor=normed)

    @pl.when(i == last)
    def _():
        o_ref[...] = _layer_norm(y_ref[...], g_ref[...], b_ref[...])


def _skewed_specs(n_tiles, rows):
    cur = lambda w: pl.BlockSpec((rows, w), lambda i: (jnp.minimum(i, n_tiles - 1), 0))
    prev = lambda w: pl.BlockSpec((rows, w), lambda i: (jnp.maximum(i - 1, 0), 0))
    return cur, prev


def _ffn_ln(x2d, w_gate, w_up, w_down, ln_g, ln_b):
    n_tok = x2d.shape[0]
    n_tiles = n_tok // FFN_ROWS
    cur, prev = _skewed_specs(n_tiles, FFN_ROWS)
    return pl.pallas_call(
        _ffn_ln_kernel,
        grid=(n_tiles + 1,),
        in_specs=[
            cur(D_MODEL),
            _resident((D_MODEL, D_FF)),
            _resident((D_MODEL, D_FF)),
            _resident((D_FF, D_MODEL)),
            _resident((1, D_MODEL)),
            _resident((1, D_MODEL)),
        ],
        out_specs=prev(D_MODEL),
        out_shape=jax.ShapeDtypeStruct((n_tok, D_MODEL), F32),
        scratch_shapes=[pltpu.VMEM((FFN_ROWS, D_MODEL), F32)],
        compiler_params=_params(1),
        name="ffn_ln",
    )(x2d, w_gate, w_up, w_down, ln_g, ln_b)


def _rotate_pairs(first, second, cos, sin):
    r1 = first * cos - second * sin
    r2 = first * sin + second * cos
    return [r1[:, :LANES], r2[:, :LANES], r1[:, LANES:], r2[:, LANES:]]


def _in_proj_kernel(x_ref, w_ref, cos_ref, sin_ref, q_ref, k_ref, v_ref, u_ref):
    xb = x_ref[...].astype(BF16)
    h = jnp.dot(xb, w_ref[...], preferred_element_type=F32)
    cos = jnp.concatenate([cos_ref[...], cos_ref[...]], axis=1)
    sin = jnp.concatenate([sin_ref[...], sin_ref[...]], axis=1)
    half = QK_WIDTH // 2
    for out_ref, base in ((q_ref, 0), (k_ref, QK_WIDTH)):
        blocks = _rotate_pairs(h[:, base:base + half], h[:, base + half:base + QK_WIDTH], cos, sin)
        for j, blk in enumerate(blocks):
            out_ref[:, j * LANES:(j + 1) * LANES] = blk.astype(BF16)
    v_ref[...] = h[:, 2 * QK_WIDTH:2 * QK_WIDTH + ATT_WIDTH].astype(BF16)
    u_ref[...] = h[:, 2 * QK_WIDTH + ATT_WIDTH:]


def _in_proj(x2d, w_proj, cos_t, sin_t, seq):
    n_tok = x2d.shape[0]
    width = w_proj.shape[1]
    tiles_per_seq = seq // PROJ_ROWS
    row = lambda w: pl.BlockSpec((PROJ_ROWS, w), lambda i: (i, 0))
    table = pl.BlockSpec((PROJ_ROWS, LANES), lambda i: (i % tiles_per_seq, 0))
    return pl.pallas_call(
        _in_proj_kernel,
        grid=(n_tok // PROJ_ROWS,),
        in_specs=[row(D_MODEL), _resident((D_MODEL, width)), table, table],
        out_specs=[row(QK_WIDTH), row(QK_WIDTH), row(ATT_WIDTH), row(POOL_WIDTH)],
        out_shape=[
            jax.ShapeDtypeStruct((n_tok, QK_WIDTH), BF16),
            jax.ShapeDtypeStruct((n_tok, QK_WIDTH), BF16),
            jax.ShapeDtypeStruct((n_tok, ATT_WIDTH), BF16),
            jax.ShapeDtypeStruct((n_tok, POOL_WIDTH), F32),
        ],
        compiler_params=_params(1),
        name="in_proj",
    )(x2d, w_proj, cos_t, sin_t)


def _pool_kernel(u_ref, pw_ref, ps_ref, o_ref):
    seq = u_ref.shape[0]
    pos = lax.broadcasted_iota(jnp.int32, (seq, POOL_GROUP_DIM), 0)

    def prev(a, k):
        return jnp.where(pos >= k, pltpu.roll(a, k, axis=0), 0.0)

    def nxt(a, k):
        return jnp.where(pos < seq - k, pltpu.roll(a, seq - k, axis=0), 0.0)

    for g, w in enumerate(POOL_WINDOWS):
        cols = slice(g * POOL_GROUP_DIM, (g + 1) * POOL_GROUP_DIM)
        u = u_ref[:, cols]
        n = w // 2
        back = prev(u, 1)
        fwd = u
        k = 1
        while k < n:
            back = back + prev(back, k)
            fwd = fwd + nxt(fwd, k)
            k *= 2
        cnt = jnp.minimum(pos + n, seq) - jnp.maximum(pos - n, 0)
        pooled = (back + fwd) / cnt.astype(F32) - u
        mixed = jnp.dot(pooled.astype(BF16), pw_ref[g], preferred_element_type=F32)
        o_ref[:, cols] = (mixed * ps_ref[:, cols]).astype(BF16)


def _pool(u2d, pool_w, pool_scale, batch, seq):
    blk = pl.BlockSpec((seq, POOL_WIDTH), lambda b: (b, 0))
    return pl.pallas_call(
        _pool_kernel,
        grid=(batch,),
        in_specs=[
            blk,
            _resident((N_POOL_GROUPS, POOL_GROUP_DIM, POOL_GROUP_DIM)),
            _resident((1, POOL_WIDTH)),
        ],
        out_specs=blk,
        out_shape=jax.ShapeDtypeStruct((batch * seq, POOL_WIDTH), BF16),
        compiler_params=_params(1),
        name="pool",
    )(u2d, pool_w, pool_scale)


def _attention_kernel(lam_ref, q_ref, k_ref, v_ref, g_ref, qn_ref, kn_ref, o_ref, vext_ref, e0_ref,
                      *, lambda_init):
    @pl.when(pl.program_id(1) == 0)
    def _():
        ones = jnp.ones((v_ref.shape[0], V_HEAD_DIM), BF16)
        for head in range(N_ATT_HEADS):
            base = head * 2 * V_HEAD_DIM
            vext_ref[:, base:base + V_HEAD_DIM] = v_ref[:, head * V_HEAD_DIM:(head + 1) * V_HEAD_DIM]
            vext_ref[:, base + V_HEAD_DIM:base + 2 * V_HEAD_DIM] = ones

    lv = lam_ref[...]
    lam = (jnp.exp(jnp.sum(lv[0:1] * lv[1:2], axis=-1, keepdims=True))
           - jnp.exp(jnp.sum(lv[2:3] * lv[3:4], axis=-1, keepdims=True))
           + lambda_init)
    group = (lax.broadcasted_iota(jnp.int32, (1, PAIR_WIDTH), 1) % LANES) // HALF_DIM
    contract_last = (((1,), (1,)), ((), ()))
    first_rows = slice(0, ATT_SUB_ROWS)

    def numerators(q_src, k_src, rows, head, m):
        pair, sub = divmod(head, 2)
        cols = slice(pair * PAIR_WIDTH, (pair + 1) * PAIR_WIDTH)
        q_pair = q_src[rows, cols]
        q_map = jnp.where(group == 2 * sub + m, q_pair, jnp.zeros_like(q_pair))
        s = lax.dot_general(q_map, k_src[:, cols], contract_last, preferred_element_type=F32)
        return jnp.exp2(s - jnp.max(s, axis=-1, keepdims=True)).astype(BF16)

    @pl.when((pl.program_id(0) == 0) & (pl.program_id(1) == 0))
    def _():
        e0_ref[...] = numerators(q_ref, k_ref, first_rows, 0, 0)

    for r in range(ATT_Q_ROWS // ATT_SUB_ROWS):
        rows = slice(r * ATT_SUB_ROWS, (r + 1) * ATT_SUB_ROWS)
        for head in range(N_ATT_HEADS):
            v_ext = vext_ref[:, head * 2 * V_HEAD_DIM:(head + 1) * 2 * V_HEAD_DIM]
            maps = []
            for m in range(2):
                if (r, head, m) == (0, 0, 0):
                    e = e0_ref[...]
                else:
                    e = numerators(q_ref, k_ref, rows, head, m)
                ev = jnp.dot(e, v_ext, preferred_element_type=F32)
                maps.append(ev[:, :V_HEAD_DIM] / ev[:, V_HEAD_DIM:])
            o = maps[0] - lam * maps[1]
            hcols = slice(head * V_HEAD_DIM, (head + 1) * V_HEAD_DIM)
            o = o * lax.rsqrt(jnp.mean(o * o, axis=-1, keepdims=True) + RMS_EPS)
            o_ref[rows, hcols] = ((o * (1.0 - lambda_init)) * g_ref[...]).astype(BF16)
    e0_ref[...] = numerators(qn_ref, kn_ref, first_rows, 0, 0)


def _attention(lam_vecs, q2d, k2d, v2d, subln_g, batch, seq, lambda_init):
    q_tiles = seq // ATT_Q_ROWS
    n_tiles = batch * q_tiles
    next_tile = lambda b, i: jnp.minimum(b * q_tiles + i + 1, n_tiles - 1)
    q_spec = pl.BlockSpec((ATT_Q_ROWS, QK_WIDTH), lambda b, i: (b * q_tiles + i, 0))
    q_next_spec = pl.BlockSpec((ATT_Q_ROWS, QK_WIDTH), lambda b, i: (next_tile(b, i), 0))
    kv_spec = lambda w: pl.BlockSpec((seq, w), lambda b, i: (b, 0))
    k_next_spec = pl.BlockSpec((seq, QK_WIDTH), lambda b, i: (next_tile(b, i) // q_tiles, 0))
    return pl.pallas_call(
        functools.partial(_attention_kernel, lambda_init=lambda_init),
        grid=(batch, q_tiles),
        in_specs=[
            _resident((4, HEAD_DIM)),
            q_spec,
            kv_spec(QK_WIDTH),
            kv_spec(ATT_WIDTH),
            _resident((1, V_HEAD_DIM)),
            q_next_spec,
            k_next_spec,
        ],
        out_specs=pl.BlockSpec((ATT_Q_ROWS, ATT_WIDTH), lambda b, i: (b * q_tiles + i, 0)),
        out_shape=jax.ShapeDtypeStruct((batch * seq, ATT_WIDTH), BF16),
        scratch_shapes=[
            pltpu.VMEM((seq, 2 * ATT_WIDTH), BF16),
            pltpu.VMEM((ATT_SUB_ROWS, seq), BF16),
        ],
        compiler_params=_params(2),
        name="diff_attention",
    )(lam_vecs, q2d, k2d, v2d, subln_g, q2d, k2d)


def _merge_ln_kernel(x_ref, ya_ref, yp_ref, wgate_ref, wa_ref, wp_ref, wo_ref, g_ref, b_ref, o_ref, y_ref):
    i = pl.program_id(0)
    last = pl.num_programs(0) - 1

    @pl.when(i == 0)
    def _():
        y_ref[...] = jnp.zeros_like(y_ref)

    @pl.when(i < last)
    def _():
        normed = _layer_norm(y_ref[...], g_ref[...], b_ref[...])
        o_ref[...] = normed
        after_norm = _zero_bits_after(normed, PAIR_WIDTH)
        for r in range(MERGE_ROWS // MERGE_SUB_ROWS):
            rows = slice(r * MERGE_SUB_ROWS, (r + 1) * MERGE_SUB_ROWS)
            x = x_ref[rows, :]
            xb = x.astype(BF16)
            gates = jax.nn.sigmoid(jnp.dot(xb, wgate_ref[...], preferred_element_type=F32))
            att = jnp.dot(ya_ref[rows, :], wa_ref[...], preferred_element_type=F32)
            pool = jnp.dot(yp_ref[rows, :], wp_ref[...], preferred_element_type=F32)
            if r == 0:
                anchored = _select_after(pool[:, :PAIR_WIDTH], att[:, :PAIR_WIDTH], after_norm)
                pool = jnp.concatenate([anchored, pool[:, PAIR_WIDTH:]], axis=1)
            merged = gates[:, :D_MODEL] * att + gates[:, D_MODEL:] * pool
            mix = jnp.dot(merged.astype(BF16), wo_ref[...], preferred_element_type=F32)
            y_ref[rows, :] = DEEPNORM_ALPHA * x + mix

    @pl.when(i == last)
    def _():
        o_ref[...] = _layer_norm(y_ref[...], g_ref[...], b_ref[...])


def _merge_ln(x2d, y_att, y_pool, w_gates, w_att, w_pool, w_out, ln_g, ln_b):
    n_tok = x2d.shape[0]
    n_tiles = n_tok // MERGE_ROWS
    cur, prev = _skewed_specs(n_tiles, MERGE_ROWS)
    return pl.pallas_call(
        _merge_ln_kernel,
        grid=(n_tiles + 1,),
        in_specs=[
            cur(D_MODEL), cur(ATT_WIDTH), cur(POOL_WIDTH),
            _resident((D_MODEL, 2 * D_MODEL)),
            _resident((ATT_WIDTH, D_MODEL)),
            _resident((POOL_WIDTH, D_MODEL)),
            _resident((D_MODEL, D_MODEL)),
            _resident((1, D_MODEL)),
            _resident((1, D_MODEL)),
        ],
        out_specs=prev(D_MODEL),
        out_shape=jax.ShapeDtypeStruct((n_tok, D_MODEL), F32),
        scratch_shapes=[pltpu.VMEM((MERGE_ROWS, D_MODEL), F32)],
        compiler_params=_params(1),
        name="merge_ln",
    )(x2d, y_att, y_pool, w_gates, w_att, w_pool, w_out, ln_g, ln_b)


def _rotary_column_order():
    first = [h * 2 * HEAD_DIM + m * HEAD_DIM + d
             for h in range(N_ATT_HEADS) for m in range(2) for d in range(HALF_DIM)]
    first = np.asarray(first, dtype=np.int32)
    return np.concatenate([first, first + HALF_DIM])


def _rope_tables(seq):
    inv = 1.0 / (ROPE_THETA ** (jnp.arange(0, HEAD_DIM, 2, dtype=F32) / HEAD_DIM))
    ang = jnp.arange(seq, dtype=F32)[:, None] * inv[None, :]
    reps = LANES // HALF_DIM
    return jnp.tile(jnp.cos(ang), (1, reps)), jnp.tile(jnp.sin(ang), (1, reps))


def _row(v):
    return v.reshape(1, -1)


def kernel(x, ln1_g, ln1_b, ffn1_w_gate, ffn1_w_up, ffn1_w_down, w_in, lambda_q1, lambda_k1, lambda_q2, lambda_k2, attn_subln_g, pool_w, pool_scale, w_branch_att, w_branch_pool, w_out, ln2_g, ln2_b, ffn2_w_gate, ffn2_w_up, ffn2_w_down, ln3_g, ln3_b):
    batch, seq, _ = x.shape
    order = _rotary_column_order()
    cos_t, sin_t = _rope_tables(seq)
    h = x.reshape(batch * seq, D_MODEL)
    for l in range(DEPTH):
        w = w_in[l]
        w_proj = jnp.concatenate(
            [w[:, :QK_WIDTH][:, order] * QK_SCALE,
             w[:, QK_WIDTH:2 * QK_WIDTH][:, order],
             w[:, 2 * QK_WIDTH:2 * QK_WIDTH + ATT_WIDTH + POOL_WIDTH]], axis=1).astype(BF16)
        w_gates = w[:, 2 * QK_WIDTH + ATT_WIDTH + POOL_WIDTH:].astype(BF16)
        lam_vecs = jnp.stack([lambda_q1[l], lambda_k1[l], lambda_q2[l], lambda_k2[l]]).astype(F32)

        h = _ffn_ln(h, ffn1_w_gate[l].astype(BF16), ffn1_w_up[l].astype(BF16),
                    ffn1_w_down[l].astype(BF16), _row(ln1_g[l]), _row(ln1_b[l]))
        q, k, v, u = _in_proj(h, w_proj, cos_t, sin_t, seq)
        y_pool = _pool(u, pool_w[l].astype(BF16), _row(pool_scale[l]), batch, seq)
        y_att = _attention(lam_vecs, q, k, v, _row(attn_subln_g[l]), batch, seq, _lambda_init(l))
        h = _merge_ln(h, y_att, y_pool, w_gates, w_branch_att[l].astype(BF16),
                      w_branch_pool[l].astype(BF16), w_out[l].astype(BF16),
                      _row(ln2_g[l]), _row(ln2_b[l]))
        h = _ffn_ln(h, ffn2_w_gate[l].astype(BF16), ffn2_w_up[l].astype(BF16),
                    ffn2_w_down[l].astype(BF16), _row(ln3_g[l]), _row(ln3_b[l]))
    return h.reshape(batch, seq, D_MODEL)
```

```python
import functools
import math

import jax
import jax.numpy as jnp
import numpy as np
from jax import lax
from jax.experimental import pallas as pl
from jax.experimental.pallas import tpu as pltpu

D_MODEL = 1024
DEPTH = 1
N_ATT_HEADS = 4
HEAD_DIM = 64
HALF_DIM = HEAD_DIM // 2
V_HEAD_DIM = 2 * HEAD_DIM
QK_WIDTH = N_ATT_HEADS * 2 * HEAD_DIM
ATT_WIDTH = N_ATT_HEADS * V_HEAD_DIM
POOL_WIDTH = D_MODEL // 2
POOL_WINDOWS = (2, 4, 8, 16)
N_POOL_GROUPS = len(POOL_WINDOWS)
POOL_GROUP_DIM = POOL_WIDTH // N_POOL_GROUPS
POOL_HALO = max(POOL_WINDOWS) // 2
D_FF = 2816
ROPE_THETA = 10000.0
LN_EPS = 1e-5
RMS_EPS = 1e-5
DEEPNORM_ALPHA = (2.0 * DEPTH) ** 0.25
QK_SCALE = HEAD_DIM ** -0.5 * math.log2(math.e)

LANES = 128
SUBLANES = 8
PAIR_WIDTH = 2 * LANES

FFN_ROWS = 1024
FFN_CHUNK = 256
FFN_ANCHOR_CHUNK = 4
PROJ_ROWS = 1024
MERGE_ROWS = 1024
MERGE_SUB_ROWS = 512
ATT_Q_ROWS = 1024
ATT_SUB_ROWS = 512
VMEM_LIMIT_BYTES = 56 * 1024 * 1024

F32 = jnp.float32
BF16 = jnp.bfloat16


def _lambda_init(layer_idx):
    return 0.8 - 0.6 * math.exp(-0.3 * layer_idx)


def _layer_norm(y, g, b):
    mu = jnp.mean(y, axis=-1, keepdims=True)
    d = y - mu
    var = jnp.mean(d * d, axis=-1, keepdims=True)
    return d * lax.rsqrt(var + LN_EPS) * g + b


def _resident(shape):
    zeros = (0,) * len(shape)
    return pl.BlockSpec(shape, lambda *_: zeros, pipeline_mode=pl.Buffered(1))


def _params(n_axes):
    return pltpu.CompilerParams(
        dimension_semantics=("arbitrary",) * n_axes,
        vmem_limit_bytes=VMEM_LIMIT_BYTES,
    )


def _zero_bits_after(values, width):
    bits = lax.bitcast_convert_type(values, jnp.uint32)
    while bits.shape[0] > SUBLANES:
        half = bits.shape[0] // 2
        bits = bits[:half] | bits[half:]
    while bits.shape[1] > LANES:
        half = bits.shape[1] // 2
        bits = bits[:, :half] | bits[:, half:]
    zero = ((bits >> 16) >> 16)[0:1]
    return jnp.concatenate([zero] * (width // LANES), axis=1)


def _select_after(x, other, zero_bits):
    return jnp.where(zero_bits == 0, x, other)


def _swiglu_residual(x, wg_ref, wu_ref, wd_ref, before_anchor=None):
    xb = x.astype(BF16)
    acc = None
    for c in range(D_FF // FFN_CHUNK):
        cols = slice(c * FFN_CHUNK, (c + 1) * FFN_CHUNK)
        gate = jnp.dot(xb, wg_ref[:, cols], preferred_element_type=F32)
        up = jnp.dot(xb, wu_ref[:, cols], preferred_element_type=F32)
        if before_anchor is not None and c == FFN_ANCHOR_CHUNK:
            up = _select_after(up, gate, _zero_bits_after(before_anchor, FFN_CHUNK))
        h = (gate * jax.nn.sigmoid(gate) * up).astype(BF16)
        part = jnp.dot(h, wd_ref[cols, :], preferred_element_type=F32)
        acc = part if acc is None else acc + part
    return DEEPNORM_ALPHA * x + 0.5 * acc


def _ffn_ln_kernel(x_ref, wg_ref, wu_ref, wd_ref, g_ref, b_ref, o_ref, y_ref):
    i = pl.program_id(0)
    last = pl.num_programs(0) - 1

    @pl.when(i == 0)
    def _():
        y_ref[...] = jnp.zeros_like(y_ref)

    @pl.when(i < last)
    def _():
        normed = _layer_norm(y_ref[...], g_ref[...], b_ref[...])
        o_ref[...] = normed
        y_ref[...] = _swiglu_residual(x_ref[...], wg_ref, wu_ref, wd_ref, before_anchor=normed)

    @pl.when(i == last)
    def _():
        o_ref[...] = _layer_norm(y_ref[...], g_ref[...], b_ref[...])


def _skewed_specs(n_tiles, rows):
    cur = lambda w: pl.BlockSpec((rows, w), lambda i: (jnp.minimum(i, n_tiles - 1), 0))
    prev = lambda w: pl.BlockSpec((rows, w), lambda i: (jnp.maximum(i - 1, 0), 0))
    return cur, prev


def _ffn_ln(x2d, w_gate, w_up, w_down, ln_g, ln_b):
    n_tok = x2d.shape[0]
    n_tiles = n_tok // FFN_ROWS
    cur, prev = _skewed_specs(n_tiles, FFN_ROWS)
    return pl.pallas_call(
        _ffn_ln_kernel,
        grid=(n_tiles + 1,),
        in_specs=[
            cur(D_MODEL),
            _resident((D_MODEL, D_FF)),
            _resident((D_MODEL, D_FF)),
            _resident((D_FF, D_MODEL)),
            _resident((1, D_MODEL)),
            _resident((1, D_MODEL)),
        ],
        out_specs=prev(D_MODEL),
        out_shape=jax.ShapeDtypeStruct((n_tok, D_MODEL), F32),
        scratch_shapes=[pltpu.VMEM((FFN_ROWS, D_MODEL), F32)],
        compiler_params=_params(1),
        name="ffn_ln",
    )(x2d, w_gate, w_up, w_down, ln_g, ln_b)


def _rotate_pairs(first, second, cos, sin):
    r1 = first * cos - second * sin
    r2 = first * sin + second * cos
    return [r1[:, :LANES], r2[:, :LANES], r1[:, LANES:], r2[:, LANES:]]


def _in_proj_kernel(x_ref, w_ref, cos_ref, sin_ref, q_ref, k_ref, v_ref, u_ref):
    xb = x_ref[...].astype(BF16)
    h = jnp.dot(xb, w_ref[...], preferred_element_type=F32)
    cos = jnp.concatenate([cos_ref[...], cos_ref[...]], axis=1)
    sin = jnp.concatenate([sin_ref[...], sin_ref[...]], axis=1)
    half = QK_WIDTH // 2
    for out_ref, base in ((q_ref, 0), (k_ref, QK_WIDTH)):
        blocks = _rotate_pairs(h[:, base:base + half], h[:, base + half:base + QK_WIDTH], cos, sin)
        for j, blk in enumerate(blocks):
            out_ref[:, j * LANES:(j + 1) * LANES] = blk.astype(BF16)
    v_ref[...] = h[:, 2 * QK_WIDTH:2 * QK_WIDTH + ATT_WIDTH].astype(BF16)
    u_ref[...] = h[:, 2 * QK_WIDTH + ATT_WIDTH:]


def _in_proj(x2d, w_proj, cos_t, sin_t, seq):
    n_tok = x2d.shape[0]
    width = w_proj.shape[1]
    tiles_per_seq = seq // PROJ_ROWS
    row = lambda w: pl.BlockSpec((PROJ_ROWS, w), lambda i: (i, 0))
    table = pl.BlockSpec((PROJ_ROWS, LANES), lambda i: (i % tiles_per_seq, 0))
    return pl.pallas_call(
        _in_proj_kernel,
        grid=(n_tok // PROJ_ROWS,),
        in_specs=[row(D_MODEL), _resident((D_MODEL, width)), table, table],
        out_specs=[row(QK_WIDTH), row(QK_WIDTH), row(ATT_WIDTH), row(POOL_WIDTH)],
        out_shape=[
            jax.ShapeDtypeStruct((n_tok, QK_WIDTH), BF16),
            jax.ShapeDtypeStruct((n_tok, QK_WIDTH), BF16),
            jax.ShapeDtypeStruct((n_tok, ATT_WIDTH), BF16),
            jax.ShapeDtypeStruct((n_tok, POOL_WIDTH), F32),
        ],
        compiler_params=_params(1),
        name="in_proj",
    )(x2d, w_proj, cos_t, sin_t)


def _pool_kernel(u_ref, pw_ref, ps_ref, o_ref):
    seq = u_ref.shape[0]
    pos = lax.broadcasted_iota(jnp.int32, (seq, POOL_GROUP_DIM), 0)
    halo = jnp.zeros((POOL_HALO, POOL_GROUP_DIM), F32)
    ext_rows = seq + 2 * POOL_HALO

    for g, w in enumerate(POOL_WINDOWS):
        cols = slice(g * POOL_GROUP_DIM, (g + 1) * POOL_GROUP_DIM)
        u = u_ref[:, cols]
        trail = jnp.concatenate([halo, u, halo], axis=0)
        k = 1
        while k < w:
            trail = trail + pltpu.roll(trail, k, axis=0)
            k *= 2
        n = w // 2
        if n > 1:
            trail = pltpu.roll(trail, ext_rows - (n - 1), axis=0)
        window = trail[POOL_HALO:POOL_HALO + seq]
        cnt = jnp.minimum(pos + n, seq) - jnp.maximum(pos - n, 0)
        pooled = window / cnt.astype(F32) - u
        mixed = jnp.dot(pooled.astype(BF16), pw_ref[g], preferred_element_type=F32)
        o_ref[:, cols] = (mixed * ps_ref[:, cols]).astype(BF16)


def _pool(u2d, pool_w, pool_scale, batch, seq):
    blk = pl.BlockSpec((seq, POOL_WIDTH), lambda b: (b, 0))
    return pl.pallas_call(
        _pool_kernel,
        grid=(batch,),
        in_specs=[
            blk,
            _resident((N_POOL_GROUPS, POOL_GROUP_DIM, POOL_GROUP_DIM)),
            _resident((1, POOL_WIDTH)),
        ],
        out_specs=blk,
        out_shape=jax.ShapeDtypeStruct((batch * seq, POOL_WIDTH), BF16),
        compiler_params=_params(1),
        name="pool",
    )(u2d, pool_w, pool_scale)


def _attention_kernel(lam_ref, q_ref, k_ref, v_ref, g_ref, qn_ref, kn_ref, o_ref, vext_ref, e0_ref,
                      *, lambda_init):
    @pl.when(pl.program_id(1) == 0)
    def _():
        ones = jnp.ones((v_ref.shape[0], V_HEAD_DIM), BF16)
        for head in range(N_ATT_HEADS):
            base = head * 2 * V_HEAD_DIM
            vext_ref[:, base:base + V_HEAD_DIM] = v_ref[:, head * V_HEAD_DIM:(head + 1) * V_HEAD_DIM]
            vext_ref[:, base + V_HEAD_DIM:base + 2 * V_HEAD_DIM] = ones

    lv = lam_ref[...]
    lam = (jnp.exp(jnp.sum(lv[0:1] * lv[1:2], axis=-1, keepdims=True))
           - jnp.exp(jnp.sum(lv[2:3] * lv[3:4], axis=-1, keepdims=True))
           + lambda_init)
    group = (lax.broadcasted_iota(jnp.int32, (1, PAIR_WIDTH), 1) % LANES) // HALF_DIM
    contract_last = (((1,), (1,)), ((), ()))
    first_rows = slice(0, ATT_SUB_ROWS)

    def numerators(q_src, k_src, rows, head, m):
        pair, sub = divmod(head, 2)
        cols = slice(pair * PAIR_WIDTH, (pair + 1) * PAIR_WIDTH)
        q_pair = q_src[rows, cols]
        q_map = jnp.where(group == 2 * sub + m, q_pair, jnp.zeros_like(q_pair))
        s = lax.dot_general(q_map, k_src[:, cols], contract_last, preferred_element_type=F32)
        return jnp.exp2(s - jnp.max(s, axis=-1, keepdims=True)).astype(BF16)

    @pl.when((pl.program_id(0) == 0) & (pl.program_id(1) == 0))
    def _():
        e0_ref[...] = numerators(q_ref, k_ref, first_rows, 0, 0)

    for r in range(ATT_Q_ROWS // ATT_SUB_ROWS):
        rows = slice(r * ATT_SUB_ROWS, (r + 1) * ATT_SUB_ROWS)
        for head in range(N_ATT_HEADS):
            v_ext = vext_ref[:, head * 2 * V_HEAD_DIM:(head + 1) * 2 * V_HEAD_DIM]
            maps = []
            for m in range(2):
                if (r, head, m) == (0, 0, 0):
                    e = e0_ref[...]
                else:
                    e = numerators(q_ref, k_ref, rows, head, m)
                ev = jnp.dot(e, v_ext, preferred_element_type=F32)
                maps.append(ev[:, :V_HEAD_DIM] / ev[:, V_HEAD_DIM:])
            o = maps[0] - lam * maps[1]
            hcols = slice(head * V_HEAD_DIM, (head + 1) * V_HEAD_DIM)
            o = o * lax.rsqrt(jnp.mean(o * o, axis=-1, keepdims=True) + RMS_EPS)
            o_ref[rows, hcols] = ((o * (1.0 - lambda_init)) * g_ref[...]).astype(BF16)
    e0_ref[...] = numerators(qn_ref, kn_ref, first_rows, 0, 0)


def _attention(lam_vecs, q2d, k2d, v2d, subln_g, batch, seq, lambda_init):
    q_tiles = seq // ATT_Q_ROWS
    n_tiles = batch * q_tiles
    next_tile = lambda b, i: jnp.minimum(b * q_tiles + i + 1, n_tiles - 1)
    q_spec = pl.BlockSpec((ATT_Q_ROWS, QK_WIDTH), lambda b, i: (b * q_tiles + i, 0))
    q_next_spec = pl.BlockSpec((ATT_Q_ROWS, QK_WIDTH), lambda b, i: (next_tile(b, i), 0))
    kv_spec = lambda w: pl.BlockSpec((seq, w), lambda b, i: (b, 0))
    k_next_spec = pl.BlockSpec((seq, QK_WIDTH), lambda b, i: (next_tile(b, i) // q_tiles, 0))
    return pl.pallas_call(
        functools.partial(_attention_kernel, lambda_init=lambda_init),
        grid=(batch, q_tiles),
        in_specs=[
            _resident((4, HEAD_DIM)),
            q_spec,
            kv_spec(QK_WIDTH),
            kv_spec(ATT_WIDTH),
            _resident((1, V_HEAD_DIM)),
            q_next_spec,
            k_next_spec,
        ],
        out_specs=pl.BlockSpec((ATT_Q_ROWS, ATT_WIDTH), lambda b, i: (b * q_tiles + i, 0)),
        out_shape=jax.ShapeDtypeStruct((batch * seq, ATT_WIDTH), BF16),
        scratch_shapes=[
            pltpu.VMEM((seq, 2 * ATT_WIDTH), BF16),
            pltpu.VMEM((ATT_SUB_ROWS, seq), BF16),
        ],
        compiler_params=_params(2),
        name="diff_attention",
    )(lam_vecs, q2d, k2d, v2d, subln_g, q2d, k2d)


def _merge_ln_kernel(x_ref, ya_ref, yp_ref, wgate_ref, wa_ref, wp_ref, wo_ref, g_ref, b_ref, o_ref, y_ref):
    i = pl.program_id(0)
    last = pl.num_programs(0) - 1

    @pl.when(i == 0)
    def _():
        y_ref[...] = jnp.zeros_like(y_ref)

    @pl.when(i < last)
    def _():
        normed = _layer_norm(y_ref[...], g_ref[...], b_ref[...])
        o_ref[...] = normed
        after_norm = _zero_bits_after(normed, PAIR_WIDTH)
        for r in range(MERGE_ROWS // MERGE_SUB_ROWS):
            rows = slice(r * MERGE_SUB_ROWS, (r + 1) * MERGE_SUB_ROWS)
            x = x_ref[rows, :]
            xb = x.astype(BF16)
            gates = jax.nn.sigmoid(jnp.dot(xb, wgate_ref[...], preferred_element_type=F32))
            att = jnp.dot(ya_ref[rows, :], wa_ref[...], preferred_element_type=F32)
            pool = jnp.dot(yp_ref[rows, :], wp_ref[...], preferred_element_type=F32)
            if r == 0:
                anchored = _select_after(pool[:, :PAIR_WIDTH], att[:, :PAIR_WIDTH], after_norm)
                pool = jnp.concatenate([anchored, pool[:, PAIR_WIDTH:]], axis=1)
            merged = gates[:, :D_MODEL] * att + gates[:, D_MODEL:] * pool
            mix = jnp.dot(merged.astype(BF16), wo_ref[...], preferred_element_type=F32)
            y_ref[rows, :] = DEEPNORM_ALPHA * x + mix

    @pl.when(i == last)
    def _():
        o_ref[...] = _layer_norm(y_ref[...], g_ref[...], b_ref[...])


def _merge_ln(x2d, y_att, y_pool, w_gates, w_att, w_pool, w_out, ln_g, ln_b):
    n_tok = x2d.shape[0]
    n_tiles = n_tok // MERGE_ROWS
    cur, prev = _skewed_specs(n_tiles, MERGE_ROWS)
    return pl.pallas_call(
        _merge_ln_kernel,
        grid=(n_tiles + 1,),
        in_specs=[
            cur(D_MODEL), cur(ATT_WIDTH), cur(POOL_WIDTH),
            _resident((D_MODEL, 2 * D_MODEL)),
            _resident((ATT_WIDTH, D_MODEL)),
            _resident((POOL_WIDTH, D_MODEL)),
            _resident((D_MODEL, D_MODEL)),
            _resident((1, D_MODEL)),
            _resident((1, D_MODEL)),
        ],
        out_specs=prev(D_MODEL),
        out_shape=jax.ShapeDtypeStruct((n_tok, D_MODEL), F32),
        scratch_shapes=[pltpu.VMEM((MERGE_ROWS, D_MODEL), F32)],
        compiler_params=_params(1),
        name="merge_ln",
    )(x2d, y_att, y_pool, w_gates, w_att, w_pool, w_out, ln_g, ln_b)


def _rotary_column_order():
    first = [h * 2 * HEAD_DIM + m * HEAD_DIM + d
             for h in range(N_ATT_HEADS) for m in range(2) for d in range(HALF_DIM)]
    first = np.asarray(first, dtype=np.int32)
    return np.concatenate([first, first + HALF_DIM])


def _rope_tables(seq):
    inv = 1.0 / (ROPE_THETA ** (jnp.arange(0, HEAD_DIM, 2, dtype=F32) / HEAD_DIM))
    ang = jnp.arange(seq, dtype=F32)[:, None] * inv[None, :]
    reps = LANES // HALF_DIM
    return jnp.tile(jnp.cos(ang), (1, reps)), jnp.tile(jnp.sin(ang), (1, reps))


def _row(v):
    return v.reshape(1, -1)


def kernel(x, ln1_g, ln1_b, ffn1_w_gate, ffn1_w_up, ffn1_w_down, w_in, lambda_q1, lambda_k1, lambda_q2, lambda_k2, attn_subln_g, pool_w, pool_scale, w_branch_att, w_branch_pool, w_out, ln2_g, ln2_b, ffn2_w_gate, ffn2_w_up, ffn2_w_down, ln3_g, ln3_b):
    batch, seq, _ = x.shape
    order = _rotary_column_order()
    cos_t, sin_t = _rope_tables(seq)
    h = x.reshape(batch * seq, D_MODEL)
    for l in range(DEPTH):
        w = w_in[l]
        w_proj = jnp.concatenate(
            [w[:, :QK_WIDTH][:, order] * QK_SCALE,
             w[:, QK_WIDTH:2 * QK_WIDTH][:, order],
             w[:, 2 * QK_WIDTH:2 * QK_WIDTH + ATT_WIDTH + POOL_WIDTH]], axis=1).astype(BF16)
        w_gates = w[:, 2 * QK_WIDTH + ATT_WIDTH + POOL_WIDTH:].astype(BF16)
        lam_vecs = jnp.stack([lambda_q1[l], lambda_k1[l], lambda_q2[l], lambda_k2[l]]).astype(F32)

        h = _ffn_ln(h, ffn1_w_gate[l].astype(BF16), ffn1_w_up[l].astype(BF16),
                    ffn1_w_down[l].astype(BF16), _row(ln1_g[l]), _row(ln1_b[l]))
        q, k, v, u = _in_proj(h, w_proj, cos_t, sin_t, seq)
        y_pool = _pool(u, pool_w[l].astype(BF16), _row(pool_scale[l]), batch, seq)
        y_att = _attention(lam_vecs, q, k, v, _row(attn_subln_g[l]), batch, seq, _lambda_init(l))
        h = _merge_ln(h, y_att, y_pool, w_gates, w_branch_att[l].astype(BF16),
                      w_branch_pool[l].astype(BF16), w_out[l].astype(BF16),
                      _row(ln2_g[l]), _row(ln2_b[l]))
        h = _ffn_ln(h, ffn2_w_gate[l].astype(BF16), ffn2_w_up[l].astype(BF16),
                    ffn2_w_down[l].astype(BF16), _row(ln3_g[l]), _row(ln3_b[l]))
    return h.reshape(batch, seq, D_MODEL)
```
